```python
import jax, jax.numpy as jnp
from jax import lax
import numpy as np

D_MODEL = 1024
BATCH = 8
SEQ = 2048
DEPTH = 4

N_MEM = 256
N_MIXERS = 2
N_POOL_LAYERS = (DEPTH + 1) // 2
N_MOBA_LAYERS = DEPTH // 2
POOL_GROUPS = 4
POOL_WINDOWS = (2, 4, 8, 16)
POOL_GW = D_MODEL // POOL_GROUPS
MOBA_HEADS = 8
HEAD_DIM = D_MODEL // MOBA_HEADS
ROT_DIM = HEAD_DIM // 4
ROPE_THETA = 500000.0
MOBA_BLOCK = 256
MOBA_TOPK = 3
Q_CHUNK = 16
MEM_HEADS = 4
MEM_HEAD_DIM = D_MODEL // MEM_HEADS
D_FF = 4 * D_MODEL
RMS_EPS = 1e-6
N_NORMS = 6

kernel_name = "pool_moba_memory_hybrid_trunk"


def rms_norm(x, g):
    xf = x.astype(jnp.float32)
    y = xf * lax.rsqrt(jnp.mean(xf * xf, axis=-1, keepdims=True) + RMS_EPS)
    return (y * g.astype(jnp.float32)).astype(x.dtype)


def apply_partial_rope(x, cos, sin):
    half = ROT_DIM // 2
    xr = x[..., :ROT_DIM].astype(jnp.float32)
    x1, x2 = xr[..., :half], xr[..., half:]
    rot = jnp.concatenate([x1 * cos - x2 * sin, x2 * cos + x1 * sin], axis=-1)
    return jnp.concatenate([rot.astype(x.dtype), x[..., ROT_DIM:]], axis=-1)


def pool_mixer(h, w_in, w_group, scale):
    b, s, _ = h.shape
    u = (h @ w_in).reshape(b, s, POOL_GROUPS, POOL_GW)
    uf = u.astype(jnp.float32)
    cs = jnp.concatenate([jnp.zeros((b, 1, POOL_GROUPS, POOL_GW), jnp.float32),
                          jnp.cumsum(uf, axis=1)], axis=1)
    t = jnp.arange(s)
    outs = []
    for g, w in enumerate(POOL_WINDOWS):
        csg = cs[:, :, g]
        cs_lo = jnp.concatenate([jnp.zeros((b, w - 1, POOL_GW), jnp.float32), csg],
                                axis=1)[:, :s]
        win_sum = csg[:, 1:] - cs_lo
        cnt = jnp.minimum(t + 1, w).astype(jnp.float32)[None, :, None]
        outs.append(win_sum / cnt - uf[:, :, g])
    pooled = jnp.stack(outs, axis=2).astype(h.dtype)
    y = jnp.einsum('bsgc,gcd->bsgd', pooled, w_group).reshape(b, s, D_MODEL)
    return y * scale


def moba_attention(h, w_qkv, w_o, cos, sin):
    b, s, _ = h.shape
    qkv = (h @ w_qkv).reshape(b, s, 3, MOBA_HEADS, HEAD_DIM)
    q = jnp.transpose(qkv[:, :, 0], (0, 2, 1, 3))
    k = jnp.transpose(qkv[:, :, 1], (0, 2, 1, 3))
    v = jnp.transpose(qkv[:, :, 2], (0, 2, 1, 3))
    q = apply_partial_rope(q, cos, sin) * (HEAD_DIM ** -0.5)
    k = apply_partial_rope(k, cos, sin)
    n_blk = -(-s // MOBA_BLOCK)
    pad = n_blk * MOBA_BLOCK - s
    kb = jnp.pad(k, ((0, 0), (0, 0), (0, pad), (0, 0))).reshape(
        b, MOBA_HEADS, n_blk, MOBA_BLOCK, HEAD_DIM)
    vb = jnp.pad(v, ((0, 0), (0, 0), (0, pad), (0, 0))).reshape(
        b, MOBA_HEADS, n_blk, MOBA_BLOCK, HEAD_DIM)
    k_mean = jnp.mean(kb.astype(jnp.float32), axis=3)
    k_sel = min(MOBA_TOPK, n_blk)
    bi = jnp.arange(b)[:, None, None, None]
    hi = jnp.arange(MOBA_HEADS)[None, :, None, None]
    blk_ids = jnp.arange(n_blk)
    key_off = jnp.arange(MOBA_BLOCK)

    def chunk(c):
        start = c * Q_CHUNK
        qc = lax.dynamic_slice_in_dim(q, start, Q_CHUNK, axis=2)
        qpos = start + jnp.arange(Q_CHUNK)
        qblk = start // MOBA_BLOCK
        gate = jnp.einsum('bhqd,bhnd->bhqn', qc.astype(jnp.float32), k_mean)
        gate = jnp.where(blk_ids < qblk, gate, -jnp.inf)
        _, sel = lax.top_k(gate, k_sel)
        valid = sel < qblk
        ks = kb[bi, hi, sel]
        vs = vb[bi, hi, sel]
        s_sel = jnp.einsum('bhqd,bhqrkd->bhqrk', qc, ks).astype(jnp.float32)
        s_sel = jnp.where(valid[..., None], s_sel, -jnp.inf).reshape(
            b, MOBA_HEADS, Q_CHUNK, k_sel * MOBA_BLOCK)
        k_own = lax.dynamic_index_in_dim(kb, qblk, axis=2, keepdims=False)
        v_own = lax.dynamic_index_in_dim(vb, qblk, axis=2, keepdims=False)
        s_own = jnp.einsum('bhqd,bhkd->bhqk', qc, k_own).astype(jnp.float32)
        own_pos = qblk * MOBA_BLOCK + key_off
        s_own = jnp.where(own_pos[None, :] <= qpos[:, None], s_own, -jnp.inf)
        p = jax.nn.softmax(jnp.concatenate([s_sel, s_own], axis=-1), axis=-1).astype(v.dtype)
        p_sel = p[..., :k_sel * MOBA_BLOCK].reshape(b, MOBA_HEADS, Q_CHUNK, k_sel, MOBA_BLOCK)
        p_own = p[..., k_sel * MOBA_BLOCK:]
        return (jnp.einsum('bhqrk,bhqrkd->bhqd', p_sel, vs)
                + jnp.einsum('bhqk,bhkd->bhqd', p_own, v_own))

    o = lax.map(chunk, jnp.arange(s // Q_CHUNK))
    o = jnp.transpose(o, (1, 0, 3, 2, 4)).reshape(b, s, D_MODEL)
    return o @ w_o


def memory_cross_attention(h, mem_n, w_q, w_kv, w_o):
    b, s, _ = h.shape
    m = mem_n.shape[1]
    q = (h @ w_q).reshape(b, s, MEM_HEADS, MEM_HEAD_DIM)
    kv = (mem_n @ w_kv).reshape(b, m, 2, MEM_HEADS, MEM_HEAD_DIM)
    k, v = kv[:, :, 0], kv[:, :, 1]
    sc = jnp.einsum('bshd,bmhd->bhsm', q, k).astype(jnp.float32) * (MEM_HEAD_DIM ** -0.5)
    p = jax.nn.softmax(sc, axis=-1).astype(v.dtype)
    o = jnp.einsum('bhsm,bmhd->bshd', p, v).reshape(b, s, D_MODEL)
    return o @ w_o


def sq_relu_mlp(h, w1, w2):
    a = jax.nn.relu(h @ w1)
    return (a * a) @ w2


def setup_inputs(seed: int = 0) -> dict:
    key = jax.random.key(seed)
    ks = jax.random.split(key, 16)
    f32 = jnp.float32

    def w(k, shape, fan_in):
        return jax.random.normal(k, shape, f32) * (fan_in ** -0.5)

    return {
        "x": jax.random.normal(ks[0], (BATCH, SEQ, D_MODEL), f32),
        "mem": jax.random.normal(ks[1], (BATCH, N_MEM, D_MODEL), f32),
        "norm_gains": 1.0 + 0.02 * jax.random.normal(ks[2], (DEPTH, N_NORMS, D_MODEL), f32),
        "mem_norm": 1.0 + 0.02 * jax.random.normal(ks[3], (DEPTH, D_MODEL), f32),
        "pool_w_in": w(ks[4], (N_POOL_LAYERS, D_MODEL, D_MODEL), D_MODEL),
        "pool_w_group": w(ks[5], (N_POOL_LAYERS, POOL_GROUPS, POOL_GW, POOL_GW), POOL_GW),
        "pool_scale": 1.0 + 0.1 * jax.random.normal(ks[6], (N_POOL_LAYERS, D_MODEL), f32),
        "moba_w_qkv": w(ks[7], (N_MOBA_LAYERS, D_MODEL, 3 * D_MODEL), D_MODEL),
        "moba_w_o": w(ks[8], (N_MOBA_LAYERS, D_MODEL, D_MODEL), D_MODEL),
        "xa_w_q": w(ks[9], (DEPTH, D_MODEL, D_MODEL), D_MODEL),
        "xa_w_kv": w(ks[10], (DEPTH, D_MODEL, 2 * D_MODEL), D_MODEL),
        "xa_w_o": w(ks[11], (DEPTH, D_MODEL, D_MODEL), D_MODEL),
        "mlp_w1": w(ks[12], (DEPTH, D_MODEL, D_FF), D_MODEL),
        "mlp_w2": w(ks[13], (DEPTH, D_FF, D_MODEL), D_FF),
    }


def reference(x, mem, norm_gains, mem_norm, pool_w_in, pool_w_group, pool_scale,
              moba_w_qkv, moba_w_o, xa_w_q, xa_w_kv, xa_w_o, mlp_w1, mlp_w2):
    s = x.shape[1]
    pos = jnp.arange(s, dtype=jnp.float32)
    inv_freq = ROPE_THETA ** (-jnp.arange(0, ROT_DIM, 2, dtype=jnp.float32) / ROT_DIM)
    ang = pos[:, None] * inv_freq[None, :]
    cos, sin = jnp.cos(ang), jnp.sin(ang)
    for i in range(DEPTH):
        j = i // N_MIXERS
        hn = rms_norm(x, norm_gains[i, 0])
        if i % N_MIXERS == 0:
            y = pool_mixer(hn, pool_w_in[j], pool_w_group[j], pool_scale[j])
        else:
            y = moba_attention(hn, moba_w_qkv[j], moba_w_o[j], cos, sin)
        x = x + rms_norm(y, norm_gains[i, 1])
        mem_n = rms_norm(mem, mem_norm[i])
        y = memory_cross_attention(rms_norm(x, norm_gains[i, 2]), mem_n,
                                   xa_w_q[i], xa_w_kv[i], xa_w_o[i])
        x = x + rms_norm(y, norm_gains[i, 3])
        y = sq_relu_mlp(rms_norm(x, norm_gains[i, 4]), mlp_w1[i], mlp_w2[i])
        x = x + rms_norm(y, norm_gains[i, 5])
    return x
```

```python
import functools

import jax
import jax.numpy as jnp
from jax import lax
from jax.experimental import pallas as pl
from jax.experimental.pallas import tpu as pltpu

F32 = jnp.float32
BF16 = jnp.bfloat16

RMS_EPS = 1e-6
POOL_WINDOWS = (2, 4, 8, 16)
POOL_GW = 256
POOL_HALO = 16
MOBA_HEADS = 8
HEAD_DIM = 128
ROT_DIM = 32
ROPE_THETA = 500000.0
MOBA_BLOCK = 256
MOBA_TOPK = 3
MEM_HEADS = 4
MEM_HEAD_DIM = 256

TOKEN_TILE = 512
FF_CHUNK = 1024
V7X_VMEM_BYTES = 64 * 1024 * 1024
V7X_VMEM_USABLE = 56 * 1024 * 1024

NT_DIMS = (((1,), (1,)), ((), ()))


def _nbytes(shape, dtype):
    n = 1
    for d in shape:
        n *= d
    return n * jnp.dtype(dtype).itemsize


def _compiler_params(semantics, pipelined_bytes, resident_bytes, temp_bytes):
    need = 2 * pipelined_bytes + resident_bytes + temp_bytes
    return pltpu.CompilerParams(
        dimension_semantics=semantics,
        vmem_limit_bytes=int(min(max(need, 16 * 1024 * 1024), V7X_VMEM_USABLE)),
    )


def _rms(xf, gain):
    ms = jnp.mean(xf * xf, axis=-1, keepdims=True)
    return xf * lax.rsqrt(ms + RMS_EPS) * gain


def _pool_kernel(x_ref, gpre_ref, gpost_ref, win_ref, wg_ref, scale_ref, o_ref, carry_ref, *, tm):
    s = pl.program_id(1)

    @pl.when(s == 0)
    def _():
        carry_ref[...] = jnp.zeros_like(carry_ref)

    x = x_ref[0]
    hn = _rms(x, gpre_ref[...]).astype(BF16)
    u = jnp.dot(hn, win_ref[...], preferred_element_type=F32)
    ext = jnp.concatenate([carry_ref[...], u], axis=0)
    carry_ref[...] = u[tm - POOL_HALO:, :]

    pos = s * tm + lax.broadcasted_iota(jnp.int32, (tm, 1), 0)
    ys = []
    for g, w in enumerate(POOL_WINDOWS):
        cols = slice(g * POOL_GW, (g + 1) * POOL_GW)
        t = ext[:, cols]
        shift = 1
        while shift < w:
            t = t + pltpu.roll(t, shift, 0)
            shift *= 2
        win_sum = t[POOL_HALO:, :]
        cnt = jnp.minimum(pos + 1, w).astype(F32)
        pooled = win_sum / cnt - u[:, cols]
        ys.append(jnp.dot(pooled.astype(BF16), wg_ref[g], preferred_element_type=F32))
    y = jnp.concatenate(ys, axis=1) * scale_ref[...]
    o_ref[0] = x + _rms(y, gpost_ref[...])


def _pool_layer(x, gpre, gpost, w_in, w_group, scale):
    b, s, d = x.shape
    tm = TOKEN_TILE
    assert s % tm == 0 and tm % 8 == 0 and tm >= POOL_HALO
    tile = _nbytes((tm, d), F32)
    weights = _nbytes(w_in.shape, BF16) + _nbytes(w_group.shape, BF16)
    return pl.pallas_call(
        functools.partial(_pool_kernel, tm=tm),
        grid=(b, s // tm),
        in_specs=[
            pl.BlockSpec((1, tm, d), lambda bi, si: (bi, si, 0)),
            pl.BlockSpec((1, d), lambda bi, si: (0, 0)),
            pl.BlockSpec((1, d), lambda bi, si: (0, 0)),
            pl.BlockSpec(w_in.shape, lambda bi, si: (0, 0)),
            pl.BlockSpec(w_group.shape, lambda bi, si: (0, 0, 0)),
            pl.BlockSpec((1, d), lambda bi, si: (0, 0)),
        ],
        out_specs=pl.BlockSpec((1, tm, d), lambda bi, si: (bi, si, 0)),
        out_shape=jax.ShapeDtypeStruct(x.shape, F32),
        scratch_shapes=[pltpu.VMEM((POOL_HALO, d), F32)],
        compiler_params=_compiler_params(("arbitrary", "arbitrary"), 2 * tile + weights, 0, 8 * tile),
        name="pool_sublayer",
    )(x, gpre, gpost, w_in, w_group, scale)


def _rope_tables(s):
    half = ROT_DIM // 2
    pos = jnp.arange(s, dtype=F32)
    inv_freq = ROPE_THETA ** (-jnp.arange(0, ROT_DIM, 2, dtype=F32) / ROT_DIM)
    ang = pos[:, None] * inv_freq[None, :]
    cos, sin = jnp.cos(ang), jnp.sin(ang)
    pad = HEAD_DIM - ROT_DIM
    c = jnp.concatenate([cos, cos, jnp.ones((s, pad), F32)], axis=1)
    s_lo = jnp.concatenate([-sin, jnp.zeros((s, HEAD_DIM - half), F32)], axis=1)
    s_hi = jnp.concatenate([jnp.zeros((s, half), F32), sin, jnp.zeros((s, pad), F32)], axis=1)
    k_tab = jnp.stack([c, s_lo, s_hi])
    return jnp.concatenate([k_tab * (HEAD_DIM ** -0.5), k_tab], axis=0)


def _qkv_kernel(x_ref, gpre_ref, w_ref, rope_ref, qkv_ref, kmean_ref, *, tm):
    s = pl.program_id(1)
    half = ROT_DIM // 2
    blocks_per_tile = tm // MOBA_BLOCK
    hn = _rms(x_ref[0], gpre_ref[...]).astype(BF16)
    for c in range(3 * MOBA_HEADS):
        if c % 2 == 0:
            y2 = jnp.dot(hn, w_ref[:, c * HEAD_DIM:(c + 2) * HEAD_DIM], preferred_element_type=F32)
        y = y2[:, (c % 2) * HEAD_DIM:(c % 2 + 1) * HEAD_DIM]
        if c < 2 * MOBA_HEADS:
            t = 0 if c < MOBA_HEADS else 3
            y = (y * rope_ref[t]
                 + pltpu.roll(y, HEAD_DIM - half, 1) * rope_ref[t + 1]
                 + pltpu.roll(y, half, 1) * rope_ref[t + 2])
        if MOBA_HEADS <= c < 2 * MOBA_HEADS:
            h = c - MOBA_HEADS
            for r in range(blocks_per_tile):
                row = jnp.mean(y[r * MOBA_BLOCK:(r + 1) * MOBA_BLOCK, :], axis=0, keepdims=True)
                kmean_ref[0, h, pl.ds(s * blocks_per_tile + r, 1), :] = row
        qkv_ref[0, c] = y.astype(BF16)


def _moba_qkv(x, gpre, w_qkv, rope):
    b, s, d = x.shape
    tm = TOKEN_TILE
    nb = s // MOBA_BLOCK
    assert s % tm == 0 and tm % MOBA_BLOCK == 0 and d == MOBA_HEADS * HEAD_DIM
    tile = _nbytes((tm, d), F32)
    out_tile = _nbytes((3 * MOBA_HEADS, tm, HEAD_DIM), BF16)
    rope_tile = _nbytes((6, tm, HEAD_DIM), F32)
    return pl.pallas_call(
        functools.partial(_qkv_kernel, tm=tm),
        grid=(b, s // tm),
        in_specs=[
            pl.BlockSpec((1, tm, d), lambda bi, si: (bi, si, 0)),
            pl.BlockSpec((1, d), lambda bi, si: (0, 0)),
            pl.BlockSpec(w_qkv.shape, lambda bi, si: (0, 0)),
            pl.BlockSpec((6, tm, HEAD_DIM), lambda bi, si: (0, si, 0)),
        ],
        out_specs=[
            pl.BlockSpec((1, 3 * MOBA_HEADS, tm, HEAD_DIM), lambda bi, si: (bi, 0, si, 0)),
            pl.BlockSpec((1, MOBA_HEADS, nb, HEAD_DIM), lambda bi, si: (bi, 0, 0, 0)),
        ],
        out_shape=[
            jax.ShapeDtypeStruct((b, 3 * MOBA_HEADS, s, HEAD_DIM), BF16),
            jax.ShapeDtypeStruct((b, MOBA_HEADS, nb, HEAD_DIM), F32),
        ],
        compiler_params=_compiler_params(
            ("arbitrary", "arbitrary"), tile + out_tile + rope_tile + _nbytes(w_qkv.shape, BF16), 0, 4 * tile),
        name="moba_qkv",
    )(x, gpre, w_qkv, rope)


def _attn_kernel(q_ref, k_ref, v_ref, km_ref, x_ref, wo_ref, gpost_ref, o_ref, oh_ref):
    i = pl.program_id(1)
    blk = MOBA_BLOCK
    nb = k_ref.shape[2] // blk

    row = lax.broadcasted_iota(jnp.int32, (blk, blk), 0)
    col = lax.broadcasted_iota(jnp.int32, (blk, blk), 1)
    causal = col <= row
    lane = lax.broadcasted_iota(jnp.int32, (blk, HEAD_DIM), 1).astype(F32)

    for ii in range(nb):
        @pl.when(i == ii)
        def _(ii=ii):
            n_past = ii * blk
            n = n_past + blk
            gated = ii > MOBA_TOPK
            if gated:
                ej = lax.broadcasted_iota(jnp.int32, (HEAD_DIM, n_past), 0)
                ec = lax.broadcasted_iota(jnp.int32, (HEAD_DIM, n_past), 1)
                expand = (lax.shift_right_logical(ec, 8) == ej).astype(BF16)

            def head(h, carry):
                q = q_ref[0, h]
                k = k_ref[0, h, :n, :]
                v = v_ref[0, h, :n, :]
                sc = lax.dot_general(q, k, NT_DIMS, preferred_element_type=F32)
                if gated:
                    km = km_ref[0, h]
                    km = jnp.concatenate([km, jnp.zeros((HEAD_DIM - nb, HEAD_DIM), F32)], axis=0)
                    km_hi = km.astype(BF16)
                    km_lo = (km - km_hi.astype(F32)).astype(BF16)
                    gate = (lax.dot_general(q, km_hi, NT_DIMS, preferred_element_type=F32)
                            + lax.dot_general(q, km_lo, NT_DIMS, preferred_element_type=F32))
                    g = jnp.where(lane < ii, gate, -jnp.inf)
                    sel = jnp.zeros((blk, HEAD_DIM), F32)
                    for _ in range(MOBA_TOPK):
                        m = jnp.max(g, axis=1, keepdims=True)
                        idx = jnp.min(jnp.where(g == m, lane, float(HEAD_DIM)), axis=1, keepdims=True)
                        hit = lane == idx
                        sel = jnp.where(hit, 1.0, sel)
                        g = jnp.where(hit, -jnp.inf, g)
                    keep_past = jnp.dot(sel.astype(BF16), expand, preferred_element_type=F32) > 0.5
                    mask = jnp.concatenate([keep_past, causal], axis=1)
                elif ii > 0:
                    mask = jnp.concatenate([jnp.ones((blk, n_past), jnp.bool_), causal], axis=1)
                else:
                    mask = causal
                sc = jnp.where(mask, sc, -jnp.inf)
                m = jnp.max(sc, axis=1, keepdims=True)
                p = jnp.exp(sc - m)
                l = jnp.sum(p, axis=1, keepdims=True)
                o = jnp.dot(p.astype(BF16), v, preferred_element_type=F32)
                oh_ref[h] = o / l
                return carry

            lax.fori_loop(0, MOBA_HEADS, head, 0)

    o_all = jnp.concatenate([oh_ref[h] for h in range(MOBA_HEADS)], axis=1).astype(BF16)
    y = jnp.dot(o_all, wo_ref[...], preferred_element_type=F32)
    o_ref[0] = x_ref[0] + _rms(y, gpost_ref[...])


def _moba_attn(x, qkv, kmean, w_o, gpost):
    b, s, d = x.shape
    blk = MOBA_BLOCK
    nb = s // blk
    h = MOBA_HEADS
    assert nb <= HEAD_DIM and blk == 256
    q_tile = _nbytes((h, blk, HEAD_DIM), BF16)
    kv_tile = _nbytes((h, s, HEAD_DIM), BF16)
    x_tile = _nbytes((blk, d), F32)
    return pl.pallas_call(
        _attn_kernel,
        grid=(b, nb),
        in_specs=[
            pl.BlockSpec((1, h, blk, HEAD_DIM), lambda bi, qi: (bi, 0, qi, 0)),
            pl.BlockSpec((1, h, s, HEAD_DIM), lambda bi, qi: (bi, 1, 0, 0)),
            pl.BlockSpec((1, h, s, HEAD_DIM), lambda bi, qi: (bi, 2, 0, 0)),
            pl.BlockSpec((1, h, nb, HEAD_DIM), lambda bi, qi: (bi, 0, 0, 0)),
            pl.BlockSpec((1, blk, d), lambda bi, qi: (bi, qi, 0)),
            pl.BlockSpec(w_o.shape, lambda bi, qi: (0, 0)),
            pl.BlockSpec((1, d), lambda bi, qi: (0, 0)),
        ],
        out_specs=pl.BlockSpec((1, blk, d), lambda bi, qi: (bi, qi, 0)),
        out_shape=jax.ShapeDtypeStruct(x.shape, F32),
        scratch_shapes=[pltpu.VMEM((h, blk, HEAD_DIM), F32)],
        compiler_params=_compiler_params(
            ("arbitrary", "arbitrary"), q_tile + 2 * kv_tile + 2 * x_tile + _nbytes(w_o.shape, BF16),
            x_tile, 6 * _nbytes((blk, s), F32)),
        name="moba_attention",
    )(qkv, qkv, qkv, kmean, x, w_o, gpost)


def _memkv_kernel(mem_ref, g_ref, w_ref, o_ref):
    mn = _rms(mem_ref[0], g_ref[0]).astype(BF16)
    o_ref[0, 0] = jnp.dot(mn, w_ref[0], preferred_element_type=F32).astype(BF16)


def _mem_kv(mem, mem_norm, w_kv):
    depth, d, d2 = w_kv.shape
    b, m, _ = mem.shape
    return pl.pallas_call(
        _memkv_kernel,
        grid=(depth, b),
        in_specs=[
            pl.BlockSpec((1, m, d), lambda li, bi: (bi, 0, 0)),
            pl.BlockSpec((1, 1, d), lambda li, bi: (li, 0, 0)),
            pl.BlockSpec((1, d, d2), lambda li, bi: (li, 0, 0)),
        ],
        out_specs=pl.BlockSpec((1, 1, m, d2), lambda li, bi: (li, bi, 0, 0)),
        out_shape=jax.ShapeDtypeStruct((depth, b, m, d2), BF16),
        compiler_params=_compiler_params(
            ("arbitrary", "arbitrary"),
            _nbytes((m, d), F32) + _nbytes((d, d2), BF16) + _nbytes((m, d2), BF16), 0, 2 * _nbytes((m, d2), F32)),
        name="memory_kv",
    )(mem, mem_norm.reshape(depth, 1, d), w_kv)


def _xattn_kernel(x_ref, gpre_ref, gpost_ref, wq_ref, kv_ref, wo_ref, o_ref):
    d = x_ref.shape[2]
    hd = MEM_HEAD_DIM
    x = x_ref[0]
    hn = _rms(x, gpre_ref[...]).astype(BF16)
    q = (jnp.dot(hn, wq_ref[...], preferred_element_type=F32) * (hd ** -0.5)).astype(BF16)
    outs = []
    for h in range(MEM_HEADS):
        qh = q[:, h * hd:(h + 1) * hd]
        kh = kv_ref[0, 0, :, h * hd:(h + 1) * hd]
        vh = kv_ref[0, 0, :, d + h * hd:d + (h + 1) * hd]
        sc = lax.dot_general(qh, kh, NT_DIMS, preferred_element_type=F32)
        m = jnp.max(sc, axis=1, keepdims=True)
        p = jnp.exp(sc - m)
        l = jnp.sum(p, axis=1, keepdims=True)
        outs.append(jnp.dot(p.astype(BF16), vh, preferred_element_type=F32) / l)
    o = jnp.concatenate(outs, axis=1).astype(BF16)
    y = jnp.dot(o, wo_ref[...], preferred_element_type=F32)
    o_ref[0] = x + _rms(y, gpost_ref[...])


def _xattn_layer(x, gpre, gpost, w_q, kv_all, layer, w_o):
    b, s, d = x.shape
    tm = TOKEN_TILE
    m = kv_all.shape[2]
    assert s % tm == 0 and d == MEM_HEADS * MEM_HEAD_DIM
    tile = _nbytes((tm, d), F32)
    return pl.pallas_call(
        _xattn_kernel,
        grid=(b, s // tm),
        in_specs=[
            pl.BlockSpec((1, tm, d), lambda bi, si: (bi, si, 0)),
            pl.BlockSpec((1, d), lambda bi, si: (0, 0)),
            pl.BlockSpec((1, d), lambda bi, si: (0, 0)),
            pl.BlockSpec(w_q.shape, lambda bi, si: (0, 0)),
            pl.BlockSpec((1, 1, m, 2 * d), lambda bi, si: (layer, bi, 0, 0)),
            pl.BlockSpec(w_o.shape, lambda bi, si: (0, 0)),
        ],
        out_specs=pl.BlockSpec((1, tm, d), lambda bi, si: (bi, si, 0)),
        out_shape=jax.ShapeDtypeStruct(x.shape, F32),
        compiler_params=_compiler_params(
            ("arbitrary", "arbitrary"),
            2 * tile + 2 * _nbytes(w_q.shape, BF16) + _nbytes((m, 2 * d), BF16), 0, 6 * tile),
        name="memory_xattn_sublayer",
    )(x, gpre, gpost, w_q, kv_all, w_o)


def _mlp_kernel(x_ref, gpre_ref, gpost_ref, w1_ref, w2_ref, o_ref):
    d_ff = w1_ref.shape[1]
    x = x_ref[...]
    hn = _rms(x, gpre_ref[...]).astype(BF16)
    acc = jnp.zeros(x.shape, F32)
    for c in range(d_ff // FF_CHUNK):
        cols = slice(c * FF_CHUNK, (c + 1) * FF_CHUNK)
        a = jnp.maximum(jnp.dot(hn, w1_ref[:, cols], preferred_element_type=F32), 0.0)
        acc = acc + jnp.dot((a * a).astype(BF16), w2_ref[cols, :], preferred_element_type=F32)
    o_ref[...] = x + _rms(acc, gpost_ref[...])


def _mlp_layer(x, gpre, gpost, w1, w2):
    b, s, d = x.shape
    tm = TOKEN_TILE
    rows = b * s
    assert rows % tm == 0 and w1.shape[1] % FF_CHUNK == 0
    tile = _nbytes((tm, d), F32)
    weights = _nbytes(w1.shape, BF16) + _nbytes(w2.shape, BF16)
    resident = pl.Buffered(1)
    out = pl.pallas_call(
        _mlp_kernel,
        grid=(rows // tm,),
        in_specs=[
            pl.BlockSpec((tm, d), lambda ri: (ri, 0)),
            pl.BlockSpec((1, d), lambda ri: (0, 0)),
            pl.BlockSpec((1, d), lambda ri: (0, 0)),
            pl.BlockSpec(w1.shape, lambda ri: (0, 0), pipeline_mode=resident),
            pl.BlockSpec(w2.shape, lambda ri: (0, 0), pipeline_mode=resident),
        ],
        out_specs=pl.BlockSpec((tm, d), lambda ri: (ri, 0)),
        out_shape=jax.ShapeDtypeStruct((rows, d), F32),
        compiler_params=_compiler_params(("arbitrary",), 2 * tile, weights, 8 * tile),
        name="mlp_sublayer",
    )(x.reshape(rows, d), gpre, gpost, w1, w2)
    return out.reshape(b, s, d)


def kernel(x, mem, norm_gains, mem_norm, pool_w_in, pool_w_group, pool_scale, moba_w_qkv, moba_w_o,
           xa_w_q, xa_w_kv, xa_w_o, mlp_w1, mlp_w2):
    depth = norm_gains.shape[0]
    s = x.shape[1]
    gain = lambda i, k: norm_gains[i, k][None, :]
    rope = _rope_tables(s)
    kv_all = _mem_kv(mem, mem_norm, xa_w_kv.astype(BF16))
    for i in range(depth):
        j = i // 2
        if i % 2 == 0:
            x = _pool_layer(x, gain(i, 0), gain(i, 1), pool_w_in[j].astype(BF16),
                            pool_w_group[j].astype(BF16), pool_scale[j][None, :])
        else:
            qkv, kmean = _moba_qkv(x, gain(i, 0), moba_w_qkv[j].astype(BF16), rope)
            x = _moba_attn(x, qkv, kmean, moba_w_o[j].astype(BF16), gain(i, 1))
        x = _xattn_layer(x, gain(i, 2), gain(i, 3), xa_w_q[i].astype(BF16), kv_all, i, xa_w_o[i].astype(BF16))
        x = _mlp_layer(x, gain(i, 4), gain(i, 5), mlp_w1[i].astype(BF16), mlp_w2[i].astype(BF16))
    return x
```

```python
import functools

import jax
import jax.numpy as jnp
from jax import lax
from jax.experimental import pallas as pl
from jax.experimental.pallas import tpu as pltpu

F32 = jnp.float32
BF16 = jnp.bfloat16

RMS_EPS = 1e-6
POOL_WINDOWS = (2, 4, 8, 16)
POOL_GW = 256
POOL_HALO = 16
MOBA_HEADS = 8
HEAD_DIM = 128
ROT_DIM = 32
ROPE_THETA = 500000.0
MOBA_BLOCK = 256
MOBA_TOPK = 3
MEM_HEADS = 4
MEM_HEAD_DIM = 256

TOKEN_TILE = 512
FF_CHUNK = 1024
V7X_VMEM_BYTES = 64 * 1024 * 1024
V7X_VMEM_USABLE = 56 * 1024 * 1024

GATE_ROWS = 16
MASK_BIAS = -1e30
SUM_ROWS = 16
LOG2_E = 1.4426950408889634

NT_DIMS = (((1,), (1,)), ((), ()))
TN_DIMS = (((0,), (0,)), ((), ()))


def _nbytes(shape, dtype):
    n = 1
    for d in shape:
        n *= d
    return n * jnp.dtype(dtype).itemsize


def _compiler_params(semantics, pipelined_bytes, resident_bytes, temp_bytes):
    need = 2 * pipelined_bytes + resident_bytes + temp_bytes
    return pltpu.CompilerParams(
        dimension_semantics=semantics,
        vmem_limit_bytes=int(min(max(need, 16 * 1024 * 1024), V7X_VMEM_USABLE)),
    )


def _rms(xf, gain):
    ms = jnp.mean(xf * xf, axis=-1, keepdims=True)
    return xf * lax.rsqrt(ms + RMS_EPS) * gain


def _pool_kernel(x_ref, gpre_ref, gpost_ref, win_ref, wg_ref, scale_ref, o_ref, carry_ref, *, tm):
    s = pl.program_id(1)

    @pl.when(s == 0)
    def _():
        carry_ref[...] = jnp.zeros_like(carry_ref)

    x = x_ref[0]
    hn = _rms(x, gpre_ref[...]).astype(BF16)
    u = jnp.dot(hn, win_ref[...], preferred_element_type=F32)
    ext = jnp.concatenate([carry_ref[...], u], axis=0)
    carry_ref[...] = u[tm - POOL_HALO:, :]

    pos = s * tm + lax.broadcasted_iota(jnp.int32, (tm, 1), 0)
    ys = []
    for g, w in enumerate(POOL_WINDOWS):
        cols = slice(g * POOL_GW, (g + 1) * POOL_GW)
        t = ext[:, cols]
        shift = 1
        while shift < w:
            t = t + pltpu.roll(t, shift, 0)
            shift *= 2
        win_sum = t[POOL_HALO:, :]
        cnt = jnp.minimum(pos + 1, w).astype(F32)
        pooled = win_sum / cnt - u[:, cols]
        ys.append(jnp.dot(pooled.astype(BF16), wg_ref[g], preferred_element_type=F32))
    y = jnp.concatenate(ys, axis=1) * scale_ref[...]
    o_ref[0] = x + _rms(y, gpost_ref[...])


def _pool_layer(x, gpre, gpost, w_in, w_group, scale):
    b, s, d = x.shape
    tm = TOKEN_TILE
    assert s % tm == 0 and tm % 8 == 0 and tm >= POOL_HALO
    tile = _nbytes((tm, d), F32)
    weights = _nbytes(w_in.shape, BF16) + _nbytes(w_group.shape, BF16)
    return pl.pallas_call(
        functools.partial(_pool_kernel, tm=tm),
        grid=(b, s // tm),
        in_specs=[
            pl.BlockSpec((1, tm, d), lambda bi, si: (bi, si, 0)),
            pl.BlockSpec((1, d), lambda bi, si: (0, 0)),
            pl.BlockSpec((1, d), lambda bi, si: (0, 0)),
            pl.BlockSpec(w_in.shape, lambda bi, si: (0, 0)),
            pl.BlockSpec(w_group.shape, lambda bi, si: (0, 0, 0)),
            pl.BlockSpec((1, d), lambda bi, si: (0, 0)),
        ],
        out_specs=pl.BlockSpec((1, tm, d), lambda bi, si: (bi, si, 0)),
        out_shape=jax.ShapeDtypeStruct(x.shape, F32),
        scratch_shapes=[pltpu.VMEM((POOL_HALO, d), F32)],
        compiler_params=_compiler_params(("arbitrary", "arbitrary"), 2 * tile + weights, 0, 8 * tile),
        name="pool_sublayer",
    )(x, gpre, gpost, w_in, w_group, scale)


def _rope_tables(s):
    half = ROT_DIM // 2
    pos = jnp.arange(s, dtype=F32)
    inv_freq = ROPE_THETA ** (-jnp.arange(0, ROT_DIM, 2, dtype=F32) / ROT_DIM)
    ang = pos[:, None] * inv_freq[None, :]
    cos, sin = jnp.cos(ang), jnp.sin(ang)
    pad = HEAD_DIM - ROT_DIM
    c = jnp.concatenate([cos, cos, jnp.ones((s, pad), F32)], axis=1)
    s_lo = jnp.concatenate([-sin, jnp.zeros((s, HEAD_DIM - half), F32)], axis=1)
    s_hi = jnp.concatenate([jnp.zeros((s, half), F32), sin, jnp.zeros((s, pad), F32)], axis=1)
    k_tab = jnp.stack([c, s_lo, s_hi])
    return jnp.concatenate([k_tab * (LOG2_E * HEAD_DIM ** -0.5), k_tab], axis=0)


def _qkv_kernel(x_ref, gpre_ref, wqk_ref, wvt_ref, rope_ref, q_ref, k2_ref, vt_ref, kmean_ref, *, tm):
    s = pl.program_id(1)
    half = ROT_DIM // 2
    blocks_per_tile = tm // MOBA_BLOCK
    hn = _rms(x_ref[0], gpre_ref[...]).astype(BF16)
    for c in range(2 * MOBA_HEADS):
        if c % 2 == 0:
            y2 = jnp.dot(hn, wqk_ref[:, c * HEAD_DIM:(c + 2) * HEAD_DIM], preferred_element_type=F32)
        y = y2[:, (c % 2) * HEAD_DIM:(c % 2 + 1) * HEAD_DIM]
        t = 0 if c < MOBA_HEADS else 3
        y = (y * rope_ref[t]
             + pltpu.roll(y, HEAD_DIM - half, 1) * rope_ref[t + 1]
             + pltpu.roll(y, half, 1) * rope_ref[t + 2])
        if c < MOBA_HEADS:
            q_ref[0, c] = y.astype(BF16)
        else:
            h = c - MOBA_HEADS
            for r in range(blocks_per_tile):
                row = jnp.mean(y[r * MOBA_BLOCK:(r + 1) * MOBA_BLOCK, :], axis=0, keepdims=True)
                kmean_ref[0, h, pl.ds(s * blocks_per_tile + r, 1), :] = row
            if h % 2 == 0:
                k_even = y
            else:
                k2_ref[0, h // 2] = jnp.concatenate([k_even, y], axis=1).astype(BF16)
    vt = lax.dot_general(wvt_ref[...], hn, NT_DIMS, preferred_element_type=F32)
    vt_ref[0] = vt.reshape(MOBA_HEADS, HEAD_DIM, tm).astype(BF16)


def _moba_qkv(x, gpre, w_qk, w_vt, rope):
    b, s, d = x.shape
    tm = TOKEN_TILE
    nb = s // MOBA_BLOCK
    h = MOBA_HEADS
    assert s % tm == 0 and tm % MOBA_BLOCK == 0 and d == h * HEAD_DIM
    tile = _nbytes((tm, d), F32)
    out_tile = _nbytes((3 * h, tm, HEAD_DIM), BF16)
    rope_tile = _nbytes((6, tm, HEAD_DIM), F32)
    weights = _nbytes(w_qk.shape, BF16) + _nbytes(w_vt.shape, BF16)
    return pl.pallas_call(
        functools.partial(_qkv_kernel, tm=tm),
        grid=(b, s // tm),
        in_specs=[
            pl.BlockSpec((1, tm, d), lambda bi, si: (bi, si, 0)),
            pl.BlockSpec((1, d), lambda bi, si: (0, 0)),
            pl.BlockSpec(w_qk.shape, lambda bi, si: (0, 0)),
            pl.BlockSpec(w_vt.shape, lambda bi, si: (0, 0)),
            pl.BlockSpec((6, tm, HEAD_DIM), lambda bi, si: (0, si, 0)),
        ],
        out_specs=[
            pl.BlockSpec((1, h, tm, HEAD_DIM), lambda bi, si: (bi, 0, si, 0)),
            pl.BlockSpec((1, h // 2, tm, 2 * HEAD_DIM), lambda bi, si: (bi, 0, si, 0)),
            pl.BlockSpec((1, h, HEAD_DIM, tm), lambda bi, si: (bi, 0, 0, si)),
            pl.BlockSpec((1, h, nb, HEAD_DIM), lambda bi, si: (bi, 0, 0, 0)),
        ],
        out_shape=[
            jax.ShapeDtypeStruct((b, h, s, HEAD_DIM), BF16),
            jax.ShapeDtypeStruct((b, h // 2, s, 2 * HEAD_DIM), BF16),
            jax.ShapeDtypeStruct((b, h, HEAD_DIM, s), BF16),
            jax.ShapeDtypeStruct((b, h, nb, HEAD_DIM), F32),
        ],
        compiler_params=_compiler_params(
            ("arbitrary", "arbitrary"), tile + out_tile + rope_tile + weights, 0, 4 * tile),
        name="moba_qkv",
    )(x, gpre, w_qk, w_vt, rope)


def _attn_kernel(q_ref, k2_ref, vt_ref, km_ref, x_ref, wo_ref, gpost_ref, o_ref, ot_ref):
    i = pl.program_id(1)
    blk = MOBA_BLOCK
    hd = HEAD_DIM
    nb = k2_ref.shape[2] // blk
    d = x_ref.shape[2]
    nbp = GATE_ROWS

    key_id = lax.broadcasted_iota(jnp.int32, (blk, 2 * blk), 0)
    qry_id = lax.broadcasted_iota(jnp.int32, (blk, 2 * blk), 1) & (blk - 1)
    causal = key_id <= qry_id
    blk_id = lax.broadcasted_iota(jnp.int32, (nbp, blk), 0)
    zeros_q = jnp.zeros((blk, hd), BF16)

    for ii in range(nb):
        @pl.when(i == ii)
        def _(ii=ii):
            n_past = ii * blk
            n = n_past + blk
            gated = ii > MOBA_TOPK
            ones = jnp.ones((SUM_ROWS, n), BF16)

            def block_bias(q, h):
                km = km_ref[0, h]
                km = jnp.concatenate([km, jnp.zeros((nbp - nb, hd), F32)], axis=0)
                km_hi = km.astype(BF16)
                km_lo = (km - km_hi.astype(F32)).astype(BF16)
                gate = (lax.dot_general(km_hi, q, NT_DIMS, preferred_element_type=F32)
                        + lax.dot_general(km_lo, q, NT_DIMS, preferred_element_type=F32))
                rank = jnp.zeros((nbp, blk), F32)
                for jp in range(ii):
                    gj = gate[jp:jp + 1, :]
                    beats = (gj > gate) | ((gj == gate) & (blk_id > jp))
                    rank = rank + beats.astype(F32)
                return jnp.where(rank >= float(MOBA_TOPK), MASK_BIAS, 0.0)

            def pair(g, carry):
                q0 = q_ref[0, 2 * g]
                q1 = q_ref[0, 2 * g + 1]
                k2 = k2_ref[0, g, :n, :]
                qd = jnp.concatenate([jnp.concatenate([q0, zeros_q], axis=1),
                                      jnp.concatenate([zeros_q, q1], axis=1)], axis=0)
                sc = lax.dot_general(k2, qd, NT_DIMS, preferred_element_type=F32)
                blocks = [sc[j * blk:(j + 1) * blk, :] for j in range(ii)]
                blocks.append(jnp.where(causal, sc[n_past:, :], -jnp.inf))
                bmax = [jnp.max(bl, axis=0, keepdims=True) for bl in blocks]
                if gated:
                    bias = jnp.concatenate([block_bias(q0, 2 * g), block_bias(q1, 2 * g + 1)], axis=1)
                    bmax = [bmax[j] + bias[j:j + 1, :] for j in range(ii)] + [bmax[ii]]
                m = functools.reduce(jnp.maximum, bmax)
                ps = []
                for j, bl in enumerate(blocks):
                    shift = bias[j:j + 1, :] - m if (gated and j < ii) else -m
                    ps.append(jnp.exp2(bl + shift).astype(BF16))
                p = jnp.concatenate(ps, axis=0)
                for hh in range(2):
                    vt = vt_ref[0, 2 * g + hh, :, :n]
                    vt1 = jnp.concatenate([vt, ones], axis=0)
                    o1 = jnp.dot(vt1, p[:, hh * blk:(hh + 1) * blk], preferred_element_type=F32)
                    ot_ref[2 * g + hh] = (o1[:hd, :] / o1[hd:hd + 1, :]).astype(BF16)
                return carry

            lax.fori_loop(0, MOBA_HEADS // 2, pair, 0)

    o_t = ot_ref[...].reshape(d, blk)
    y = lax.dot_general(o_t, wo_ref[...], TN_DIMS, preferred_element_type=F32)
    o_ref[0] = x_ref[0] + _rms(y, gpost_ref[...])


def _moba_attn(x, q, k2, vt, kmean, w_o, gpost):
    b, s, d = x.shape
    blk = MOBA_BLOCK
    nb = s // blk
    h = MOBA_HEADS
    assert nb <= GATE_ROWS and blk == 256 and h % 2 == 0
    q_tile = _nbytes((h, blk, HEAD_DIM), BF16)
    kv_tile = _nbytes((h, s, HEAD_DIM), BF16)
    x_tile = _nbytes((blk, d), F32)
    return pl.pallas_call(
        _attn_kernel,
        grid=(b, nb),
        in_specs=[
            pl.BlockSpec((1, h, blk, HEAD_DIM), lambda bi, qi: (bi, 0, qi, 0)),
            pl.BlockSpec((1, h // 2, s, 2 * HEAD_DIM), lambda bi, qi: (bi, 0, 0, 0)),
            pl.BlockSpec((1, h, HEAD_DIM, s), lambda bi, qi: (bi, 0, 0, 0)),
            pl.BlockSpec((1, h, nb, HEAD_DIM), lambda bi, qi: (bi, 0, 0, 0)),
            pl.BlockSpec((1, blk, d), lambda bi, qi: (bi, qi, 0)),
            pl.BlockSpec(w_o.shape, lambda bi, qi: (0, 0)),
            pl.BlockSpec((1, d), lambda bi, qi: (0, 0)),
        ],
        out_specs=pl.BlockSpec((1, blk, d), lambda bi, qi: (bi, qi, 0)),
        out_shape=jax.ShapeDtypeStruct(x.shape, F32),
        scratch_shapes=[pltpu.VMEM((h, HEAD_DIM, blk), BF16)],
        compiler_params=_compiler_params(
            ("arbitrary", "arbitrary"), q_tile + 2 * kv_tile + 2 * x_tile + _nbytes(w_o.shape, BF16),
            x_tile, 8 * _nbytes((blk, s), F32)),
        name="moba_attention",
    )(q, k2, vt, kmean, x, w_o, gpost)


def _memkv_kernel(mem_ref, g_ref, w_ref, o_ref):
    mn = _rms(mem_ref[0], g_ref[0]).astype(BF16)
    o_ref[0, 0] = jnp.dot(mn, w_ref[0], preferred_element_type=F32).astype(BF16)


def _mem_kv(mem, mem_norm, w_kv):
    depth, d, d2 = w_kv.shape
    b, m, _ = mem.shape
    return pl.pallas_call(
        _memkv_kernel,
        grid=(depth, b),
        in_specs=[
            pl.BlockSpec((1, m, d), lambda li, bi: (bi, 0, 0)),
            pl.BlockSpec((1, 1, d), lambda li, bi: (li, 0, 0)),
            pl.BlockSpec((1, d, d2), lambda li, bi: (li, 0, 0)),
        ],
        out_specs=pl.BlockSpec((1, 1, m, d2), lambda li, bi: (li, bi, 0, 0)),
        out_shape=jax.ShapeDtypeStruct((depth, b, m, d2), BF16),
        compiler_params=_compiler_params(
            ("arbitrary", "arbitrary"),
            _nbytes((m, d), F32) + _nbytes((d, d2), BF16) + _nbytes((m, d2), BF16), 0, 2 * _nbytes((m, d2), F32)),
        name="memory_kv",
    )(mem, mem_norm.reshape(depth, 1, d), w_kv)


def _xattn_kernel(x_ref, gpre_ref, gpost_ref, wq_ref, kv_ref, wo_ref, o_ref):
    d = x_ref.shape[2]
    hd = MEM_HEAD_DIM
    x = x_ref[0]
    hn = _rms(x, gpre_ref[...]).astype(BF16)
    q = (jnp.dot(hn, wq_ref[...], preferred_element_type=F32) * (hd ** -0.5)).astype(BF16)
    outs = []
    for h in range(MEM_HEADS):
        qh = q[:, h * hd:(h + 1) * hd]
        kh = kv_ref[0, 0, :, h * hd:(h + 1) * hd]
        vh = kv_ref[0, 0, :, d + h * hd:d + (h + 1) * hd]
        sc = lax.dot_general(qh, kh, NT_DIMS, preferred_element_type=F32)
        m = jnp.max(sc, axis=1, keepdims=True)
        p = jnp.exp(sc - m)
        l = jnp.sum(p, axis=1, keepdims=True)
        outs.append(jnp.dot(p.astype(BF16), vh, preferred_element_type=F32) / l)
    o = jnp.concatenate(outs, axis=1).astype(BF16)
    y = jnp.dot(o, wo_ref[...], preferred_element_type=F32)
    o_ref[0] = x + _rms(y, gpost_ref[...])


def _xattn_layer(x, gpre, gpost, w_q, kv_all, layer, w_o):
    b, s, d = x.shape
    tm = TOKEN_TILE
    m = kv_all.shape[2]
    assert s % tm == 0 and d == MEM_HEADS * MEM_HEAD_DIM
    tile = _nbytes((tm, d), F32)
    return pl.pallas_call(
        _xattn_kernel,
        grid=(b, s // tm),
        in_specs=[
            pl.BlockSpec((1, tm, d), lambda bi, si: (bi, si, 0)),
            pl.BlockSpec((1, d), lambda bi, si: (0, 0)),
            pl.BlockSpec((1, d), lambda bi, si: (0, 0)),
            pl.BlockSpec(w_q.shape, lambda bi, si: (0, 0)),
            pl.BlockSpec((1, 1, m, 2 * d), lambda bi, si: (layer, bi, 0, 0)),
            pl.BlockSpec(w_o.shape, lambda bi, si: (0, 0)),
        ],
        out_specs=pl.BlockSpec((1, tm, d), lambda bi, si: (bi, si, 0)),
        out_shape=jax.ShapeDtypeStruct(x.shape, F32),
        compiler_params=_compiler_params(
            ("arbitrary", "arbitrary"),
            2 * tile + 2 * _nbytes(w_q.shape, BF16) + _nbytes((m, 2 * d), BF16), 0, 6 * tile),
        name="memory_xattn_sublayer",
    )(x, gpre, gpost, w_q, kv_all, w_o)


def _mlp_kernel(x_ref, gpre_ref, gpost_ref, w1_ref, w2_ref, o_ref):
    d_ff = w1_ref.shape[1]
    x = x_ref[...]
    hn = _rms(x, gpre_ref[...]).astype(BF16)
    acc = jnp.zeros(x.shape, F32)
    for c in range(d_ff // FF_CHUNK):
        cols = slice(c * FF_CHUNK, (c + 1) * FF_CHUNK)
        a = jnp.maximum(jnp.dot(hn, w1_ref[:, cols], preferred_element_type=F32), 0.0)
        acc = acc + jnp.dot((a * a).astype(BF16), w2_ref[cols, :], preferred_element_type=F32)
    o_ref[...] = x + _rms(acc, gpost_ref[...])


def _mlp_layer(x, gpre, gpost, w1, w2):
    b, s, d = x.shape
    tm = TOKEN_TILE
    rows = b * s
    assert rows % tm == 0 and w1.shape[1] % FF_CHUNK == 0
    tile = _nbytes((tm, d), F32)
    weights = _nbytes(w1.shape, BF16) + _nbytes(w2.shape, BF16)
    resident = pl.Buffered(1)
    out = pl.pallas_call(
        _mlp_kernel,
        grid=(rows // tm,),
        in_specs=[
            pl.BlockSpec((tm, d), lambda ri: (ri, 0)),
            pl.BlockSpec((1, d), lambda ri: (0, 0)),
            pl.BlockSpec((1, d), lambda ri: (0, 0)),
            pl.BlockSpec(w1.shape, lambda ri: (0, 0), pipeline_mode=resident),
            pl.BlockSpec(w2.shape, lambda ri: (0, 0), pipeline_mode=resident),
        ],
        out_specs=pl.BlockSpec((tm, d), lambda ri: (ri, 0)),
        out_shape=jax.ShapeDtypeStruct((rows, d), F32),
        compiler_params=_compiler_params(("arbitrary",), 2 * tile, weights, 8 * tile),
        name="mlp_sublayer",
    )(x.reshape(rows, d), gpre, gpost, w1, w2)
    return out.reshape(b, s, d)


def kernel(x, mem, norm_gains, mem_norm, pool_w_in, pool_w_group, pool_scale, moba_w_qkv, moba_w_o,
           xa_w_q, xa_w_kv, xa_w_o, mlp_w1, mlp_w2):
    depth = norm_gains.shape[0]
    s, d = x.shape[1], x.shape[2]
    gain = lambda i, k: norm_gains[i, k][None, :]
    rope = _rope_tables(s)
    kv_all = _mem_kv(mem, mem_norm, xa_w_kv.astype(BF16))
    for i in range(depth):
        j = i // 2
        if i % 2 == 0:
            x = _pool_layer(x, gain(i, 0), gain(i, 1), pool_w_in[j].astype(BF16),
                            pool_w_group[j].astype(BF16), pool_scale[j][None, :])
        else:
            w_qk = moba_w_qkv[j][:, :2 * d].astype(BF16)
            w_vt = moba_w_qkv[j][:, 2 * d:].T.astype(BF16)
            q, k2, vt, kmean = _moba_qkv(x, gain(i, 0), w_qk, w_vt, rope)
            x = _moba_attn(x, q, k2, vt, kmean, moba_w_o[j].astype(BF16), gain(i, 1))
        x = _xattn_layer(x, gain(i, 2), gain(i, 3), xa_w_q[i].astype(BF16), kv_all, i, xa_w_o[i].astype(BF16))
        x = _mlp_layer(x, gain(i, 4), gain(i, 5), mlp_w1[i].astype(BF16), mlp_w2[i].astype(BF16))
    return x
```

```python
import functools

import jax
import jax.numpy as jnp
from jax import lax
from jax.experimental import pallas as pl
from jax.experimental.pallas import tpu as pltpu

F32 = jnp.float32
BF16 = jnp.bfloat16

RMS_EPS = 1e-6
POOL_WINDOWS = (2, 4, 8, 16)
POOL_GW = 256
POOL_HALO = 16
MOBA_HEADS = 8
HEAD_DIM = 128
ROT_DIM = 32
ROPE_THETA = 500000.0
MOBA_BLOCK = 256
MOBA_TOPK = 3
MEM_HEADS = 4
MEM_HEAD_DIM = 256

TOKEN_TILE = 512
FF_CHUNK = 1024
V7X_VMEM_BYTES = 64 * 1024 * 1024
V7X_VMEM_USABLE = 56 * 1024 * 1024

GATE_ROWS = 16
MASK_BIAS = -1e30
SUM_ROWS = 16
LOG2_E = 1.4426950408889634

NT_DIMS = (((1,), (1,)), ((), ()))
TN_DIMS = (((0,), (0,)), ((), ()))


def _nbytes(shape, dtype):
    n = 1
    for d in shape:
        n *= d
    return n * jnp.dtype(dtype).itemsize


def _compiler_params(semantics, pipelined_bytes, resident_bytes, temp_bytes):
    need = 2 * pipelined_bytes + resident_bytes + temp_bytes
    return pltpu.CompilerParams(
        dimension_semantics=semantics,
        vmem_limit_bytes=int(min(max(need, 16 * 1024 * 1024), V7X_VMEM_USABLE)),
    )


def _rms(xf, gain):
    ms = jnp.mean(xf * xf, axis=-1, keepdims=True)
    return xf * lax.rsqrt(ms + RMS_EPS) * gain


def _pool_kernel(x_ref, gpre_ref, gpost_ref, win_ref, wg_ref, scale_ref, o_ref, carry_ref, *, tm):
    s = pl.program_id(1)

    @pl.when(s == 0)
    def _():
        carry_ref[...] = jnp.zeros_like(carry_ref)

    x = x_ref[0]
    hn = _rms(x, gpre_ref[...]).astype(BF16)
    u = jnp.dot(hn, win_ref[...], preferred_element_type=F32)
    ext = jnp.concatenate([carry_ref[...], u], axis=0)
    carry_ref[...] = u[tm - POOL_HALO:, :]

    pos = s * tm + lax.broadcasted_iota(jnp.int32, (tm, 1), 0)
    ys = []
    for g, w in enumerate(POOL_WINDOWS):
        cols = slice(g * POOL_GW, (g + 1) * POOL_GW)
        t = ext[:, cols]
        shift = 1
        while shift < w:
            t = t + pltpu.roll(t, shift, 0)
            shift *= 2
        win_sum = t[POOL_HALO:, :]
        cnt = jnp.minimum(pos + 1, w).astype(F32)
        pooled = win_sum / cnt - u[:, cols]
        ys.append(jnp.dot(pooled.astype(BF16), wg_ref[g], preferred_element_type=F32))
    y = jnp.concatenate(ys, axis=1) * scale_ref[...]
    o_ref[0] = x + _rms(y, gpost_ref[...])


def _pool_layer(x, gpre, gpost, w_in, w_group, scale):
    b, s, d = x.shape
    tm = TOKEN_TILE
    assert s % tm == 0 and tm % 8 == 0 and tm >= POOL_HALO
    tile = _nbytes((tm, d), F32)
    weights = _nbytes(w_in.shape, BF16) + _nbytes(w_group.shape, BF16)
    return pl.pallas_call(
        functools.partial(_pool_kernel, tm=tm),
        grid=(b, s // tm),
        in_specs=[
            pl.BlockSpec((1, tm, d), lambda bi, si: (bi, si, 0)),
            pl.BlockSpec((1, d), lambda bi, si: (0, 0)),
            pl.BlockSpec((1, d), lambda bi, si: (0, 0)),
            pl.BlockSpec(w_in.shape, lambda bi, si: (0, 0)),
            pl.BlockSpec(w_group.shape, lambda bi, si: (0, 0, 0)),
            pl.BlockSpec((1, d), lambda bi, si: (0, 0)),
        ],
        out_specs=pl.BlockSpec((1, tm, d), lambda bi, si: (bi, si, 0)),
        out_shape=jax.ShapeDtypeStruct(x.shape, F32),
        scratch_shapes=[pltpu.VMEM((POOL_HALO, d), F32)],
        compiler_params=_compiler_params(("arbitrary", "arbitrary"), 2 * tile + weights, 0, 8 * tile),
        name="pool_sublayer",
    )(x, gpre, gpost, w_in, w_group, scale)


def _rope_tables(s):
    half = ROT_DIM // 2
    pos = jnp.arange(s, dtype=F32)
    inv_freq = ROPE_THETA ** (-jnp.arange(0, ROT_DIM, 2, dtype=F32) / ROT_DIM)
    ang = pos[:, None] * inv_freq[None, :]
    cos, sin = jnp.cos(ang), jnp.sin(ang)
    pad = HEAD_DIM - ROT_DIM
    c = jnp.concatenate([cos, cos, jnp.ones((s, pad), F32)], axis=1)
    s_lo = jnp.concatenate([-sin, jnp.zeros((s, HEAD_DIM - half), F32)], axis=1)
    s_hi = jnp.concatenate([jnp.zeros((s, half), F32), sin, jnp.zeros((s, pad), F32)], axis=1)
    k_tab = jnp.stack([c, s_lo, s_hi])
    return jnp.concatenate([k_tab * (LOG2_E * HEAD_DIM ** -0.5), k_tab], axis=0)


def _qkv_kernel(x_ref, gpre_ref, wqk_ref, wvt_ref, rope_ref, q_ref, k2_ref, vt_ref, kmean_ref, *, tm):
    s = pl.program_id(1)
    half = ROT_DIM // 2
    blocks_per_tile = tm // MOBA_BLOCK
    hn = _rms(x_ref[0], gpre_ref[...]).astype(BF16)
    for c in range(2 * MOBA_HEADS):
        if c % 2 == 0:
            y2 = jnp.dot(hn, wqk_ref[:, c * HEAD_DIM:(c + 2) * HEAD_DIM], preferred_element_type=F32)
        y = y2[:, (c % 2) * HEAD_DIM:(c % 2 + 1) * HEAD_DIM]
        t = 0 if c < MOBA_HEADS else 3
        y = (y * rope_ref[t]
             + pltpu.roll(y, HEAD_DIM - half, 1) * rope_ref[t + 1]
             + pltpu.roll(y, half, 1) * rope_ref[t + 2])
        if c < MOBA_HEADS:
            q_ref[0, c] = y.astype(BF16)
        else:
            h = c - MOBA_HEADS
            for r in range(blocks_per_tile):
                row = jnp.mean(y[r * MOBA_BLOCK:(r + 1) * MOBA_BLOCK, :], axis=0, keepdims=True)
                kmean_ref[0, h, pl.ds(s * blocks_per_tile + r, 1), :] = row
            if h % 2 == 0:
                k_even = y
            else:
                k2_ref[0, h // 2] = jnp.concatenate([k_even, y], axis=1).astype(BF16)
    vt = lax.dot_general(wvt_ref[...], hn, NT_DIMS, preferred_element_type=F32)
    vt_ref[0] = vt.reshape(MOBA_HEADS, HEAD_DIM, tm).astype(BF16)


def _moba_qkv(x, gpre, w_qk, w_vt, rope):
    b, s, d = x.shape
    tm = TOKEN_TILE
    nb = s // MOBA_BLOCK
    h = MOBA_HEADS
    assert s % tm == 0 and tm % MOBA_BLOCK == 0 and d == h * HEAD_DIM
    tile = _nbytes((tm, d), F32)
    out_tile = _nbytes((3 * h, tm, HEAD_DIM), BF16)
    rope_tile = _nbytes((6, tm, HEAD_DIM), F32)
    weights = _nbytes(w_qk.shape, BF16) + _nbytes(w_vt.shape, BF16)
    return pl.pallas_call(
        functools.partial(_qkv_kernel, tm=tm),
        grid=(b, s // tm),
        in_specs=[
            pl.BlockSpec((1, tm, d), lambda bi, si: (bi, si, 0)),
            pl.BlockSpec((1, d), lambda bi, si: (0, 0)),
            pl.BlockSpec(w_qk.shape, lambda bi, si: (0, 0)),
            pl.BlockSpec(w_vt.shape, lambda bi, si: (0, 0)),
            pl.BlockSpec((6, tm, HEAD_DIM), lambda bi, si: (0, si, 0)),
        ],
        out_specs=[
            pl.BlockSpec((1, h, tm, HEAD_DIM), lambda bi, si: (bi, 0, si, 0)),
            pl.BlockSpec((1, h // 2, tm, 2 * HEAD_DIM), lambda bi, si: (bi, 0, si, 0)),
            pl.BlockSpec((1, h, HEAD_DIM, tm), lambda bi, si: (bi, 0, 0, si)),
            pl.BlockSpec((1, h, nb, HEAD_DIM), lambda bi, si: (bi, 0, 0, 0)),
        ],
        out_shape=[
            jax.ShapeDtypeStruct((b, h, s, HEAD_DIM), BF16),
            jax.ShapeDtypeStruct((b, h // 2, s, 2 * HEAD_DIM), BF16),
            jax.ShapeDtypeStruct((b, h, HEAD_DIM, s), BF16),
            jax.ShapeDtypeStruct((b, h, nb, HEAD_DIM), F32),
        ],
        compiler_params=_compiler_params(
            ("arbitrary", "arbitrary"), tile + out_tile + rope_tile + weights, 0, 4 * tile),
        name="moba_qkv",
    )(x, gpre, w_qk, w_vt, rope)


def _attn_kernel(q_ref, k2_ref, vt_ref, km_ref, x_ref, wo_ref, gpost_ref, o_ref, ot_ref):
    i = pl.program_id(1)
    blk = MOBA_BLOCK
    hd = HEAD_DIM
    nb = k2_ref.shape[2] // blk
    d = x_ref.shape[2]
    nbp = GATE_ROWS

    key_id = lax.broadcasted_iota(jnp.int32, (blk, 2 * blk), 0)
    qry_id = lax.broadcasted_iota(jnp.int32, (blk, 2 * blk), 1) & (blk - 1)
    causal = key_id <= qry_id
    blk_id = lax.broadcasted_iota(jnp.int32, (nbp, blk), 0)
    zeros_q = jnp.zeros((blk, hd), BF16)

    for ii in range(nb):
        @pl.when(i == ii)
        def _(ii=ii):
            n_past = ii * blk
            n = n_past + blk
            gated = ii > MOBA_TOPK
            ones = jnp.ones((SUM_ROWS, blk), BF16)

            def block_bias(q, h):
                km = km_ref[0, h]
                km = jnp.concatenate([km, jnp.zeros((nbp - nb, hd), F32)], axis=0)
                km_hi = km.astype(BF16)
                km_lo = (km - km_hi.astype(F32)).astype(BF16)
                gate = (lax.dot_general(km_hi, q, NT_DIMS, preferred_element_type=F32)
                        + lax.dot_general(km_lo, q, NT_DIMS, preferred_element_type=F32))
                rank = jnp.zeros((nbp, blk), F32)
                for jp in range(ii):
                    gj = gate[jp:jp + 1, :]
                    beats = (gj > gate) | ((gj == gate) & (blk_id > jp))
                    rank = rank + beats.astype(F32)
                return jnp.where(rank >= float(MOBA_TOPK), MASK_BIAS, 0.0)

            def pair(g, carry):
                q0 = q_ref[0, 2 * g]
                q1 = q_ref[0, 2 * g + 1]
                qd = jnp.concatenate([jnp.concatenate([q0, zeros_q], axis=1),
                                      jnp.concatenate([zeros_q, q1], axis=1)], axis=0)
                if gated:
                    bias = jnp.concatenate([block_bias(q0, 2 * g), block_bias(q1, 2 * g + 1)], axis=1)
                m = None
                acc = [None, None]
                order = [ii] + list(range(ii))

                def scores(j):
                    sc = lax.dot_general(k2_ref[0, g, j * blk:(j + 1) * blk, :], qd, NT_DIMS,
                                         preferred_element_type=F32)
                    if j == ii:
                        sc = jnp.where(causal, sc, -jnp.inf)
                    bm = jnp.max(sc, axis=0, keepdims=True)
                    if gated and j < ii:
                        bm = bm + bias[j:j + 1, :]
                    return sc, bm

                def accumulate(j, p, alpha):
                    for hh in range(2):
                        cols = slice(hh * blk, (hh + 1) * blk)
                        vt1 = jnp.concatenate([vt_ref[0, 2 * g + hh, :, j * blk:(j + 1) * blk], ones], axis=0)
                        pv = jnp.dot(vt1, p[:, cols], preferred_element_type=F32)
                        acc[hh] = pv if alpha is None else acc[hh] * alpha[:, cols] + pv

                n_blk = len(order)
                sc_q = [scores(order[0])]
                if n_blk > 1:
                    sc_q.append(scores(order[1]))
                pending = None
                for t, j in enumerate(order):
                    sc, bm = sc_q.pop(0)
                    if t + 2 < n_blk:
                        sc_q.append(scores(order[t + 2]))
                    if pending is not None:
                        accumulate(*pending)
                    masked = gated and j < ii
                    m_new = bm if m is None else jnp.maximum(m, bm)
                    shift = bias[j:j + 1, :] - m_new if masked else -m_new
                    p = jnp.exp2(sc + shift).astype(BF16)
                    alpha = None if m is None else jnp.exp2(m - m_new)
                    pending = (j, p, alpha)
                    m = m_new
                accumulate(*pending)
                for hh in range(2):
                    ot_ref[2 * g + hh] = (acc[hh][:hd, :] / acc[hh][hd:hd + 1, :]).astype(BF16)
                return carry

            lax.fori_loop(0, MOBA_HEADS // 2, pair, 0, unroll=2)

    o_t = ot_ref[...].reshape(d, blk)
    y = lax.dot_general(o_t, wo_ref[...], TN_DIMS, preferred_element_type=F32)
    o_ref[0] = x_ref[0] + _rms(y, gpost_ref[...])


def _moba_attn(x, q, k2, vt, kmean, w_o, gpost):
    b, s, d = x.shape
    blk = MOBA_BLOCK
    nb = s // blk
    h = MOBA_HEADS
    assert nb <= GATE_ROWS and blk == 256 and h % 2 == 0
    q_tile = _nbytes((h, blk, HEAD_DIM), BF16)
    kv_tile = _nbytes((h, s, HEAD_DIM), BF16)
    x_tile = _nbytes((blk, d), F32)
    return pl.pallas_call(
        _attn_kernel,
        grid=(b, nb),
        in_specs=[
            pl.BlockSpec((1, h, blk, HEAD_DIM), lambda bi, qi: (bi, 0, qi, 0)),
            pl.BlockSpec((1, h // 2, s, 2 * HEAD_DIM), lambda bi, qi: (bi, 0, 0, 0)),
            pl.BlockSpec((1, h, HEAD_DIM, s), lambda bi, qi: (bi, 0, 0, 0)),
            pl.BlockSpec((1, h, nb, HEAD_DIM), lambda bi, qi: (bi, 0, 0, 0)),
            pl.BlockSpec((1, blk, d), lambda bi, qi: (bi, qi, 0)),
            pl.BlockSpec(w_o.shape, lambda bi, qi: (0, 0)),
            pl.BlockSpec((1, d), lambda bi, qi: (0, 0)),
        ],
        out_specs=pl.BlockSpec((1, blk, d), lambda bi, qi: (bi, qi, 0)),
        out_shape=jax.ShapeDtypeStruct(x.shape, F32),
        scratch_shapes=[pltpu.VMEM((h, HEAD_DIM, blk), BF16)],
        compiler_params=_compiler_params(
            ("arbitrary", "arbitrary"), q_tile + 2 * kv_tile + 2 * x_tile + _nbytes(w_o.shape, BF16),
            x_tile, 8 * _nbytes((blk, s), F32)),
        name="moba_attention",
    )(q, k2, vt, kmean, x, w_o, gpost)


def _memkv_kernel(mem_ref, g_ref, w_ref, o_ref):
    mn = _rms(mem_ref[0], g_ref[0]).astype(BF16)
    o_ref[0, 0] = jnp.dot(mn, w_ref[0], preferred_element_type=F32).astype(BF16)


def _mem_kv(mem, mem_norm, w_kv):
    depth, d, d2 = w_kv.shape
    b, m, _ = mem.shape
    return pl.pallas_call(
        _memkv_kernel,
        grid=(depth, b),
        in_specs=[
            pl.BlockSpec((1, m, d), lambda li, bi: (bi, 0, 0)),
            pl.BlockSpec((1, 1, d), lambda li, bi: (li, 0, 0)),
            pl.BlockSpec((1, d, d2), lambda li, bi: (li, 0, 0)),
        ],
        out_specs=pl.BlockSpec((1, 1, m, d2), lambda li, bi: (li, bi, 0, 0)),
        out_shape=jax.ShapeDtypeStruct((depth, b, m, d2), BF16),
        compiler_params=_compiler_params(
            ("arbitrary", "arbitrary"),
            _nbytes((m, d), F32) + _nbytes((d, d2), BF16) + _nbytes((m, d2), BF16), 0, 2 * _nbytes((m, d2), F32)),
        name="memory_kv",
    )(mem, mem_norm.reshape(depth, 1, d), w_kv)


def _xattn_kernel(x_ref, gpre_ref, gpost_ref, wq_ref, kv_ref, wo_ref, o_ref):
    d = x_ref.shape[2]
    hd = MEM_HEAD_DIM
    x = x_ref[0]
    hn = _rms(x, gpre_ref[...]).astype(BF16)
    q = (jnp.dot(hn, wq_ref[...], preferred_element_type=F32) * (hd ** -0.5)).astype(BF16)
    outs = []
    for h in range(MEM_HEADS):
        qh = q[:, h * hd:(h + 1) * hd]
        kh = kv_ref[0, 0, :, h * hd:(h + 1) * hd]
        vh = kv_ref[0, 0, :, d + h * hd:d + (h + 1) * hd]
        sc = lax.dot_general(qh, kh, NT_DIMS, preferred_element_type=F32)
        m = jnp.max(sc, axis=1, keepdims=True)
        p = jnp.exp(sc - m)
        l = jnp.sum(p, axis=1, keepdims=True)
        outs.append(jnp.dot(p.astype(BF16), vh, preferred_element_type=F32) / l)
    o = jnp.concatenate(outs, axis=1).astype(BF16)
    y = jnp.dot(o, wo_ref[...], preferred_element_type=F32)
    o_ref[0] = x + _rms(y, gpost_ref[...])


def _xattn_layer(x, gpre, gpost, w_q, kv_all, layer, w_o):
    b, s, d = x.shape
    tm = TOKEN_TILE
    m = kv_all.shape[2]
    assert s % tm == 0 and d == MEM_HEADS * MEM_HEAD_DIM
    tile = _nbytes((tm, d), F32)
    return pl.pallas_call(
        _xattn_kernel,
        grid=(b, s // tm),
        in_specs=[
            pl.BlockSpec((1, tm, d), lambda bi, si: (bi, si, 0)),
            pl.BlockSpec((1, d), lambda bi, si: (0, 0)),
            pl.BlockSpec((1, d), lambda bi, si: (0, 0)),
            pl.BlockSpec(w_q.shape, lambda bi, si: (0, 0)),
            pl.BlockSpec((1, 1, m, 2 * d), lambda bi, si: (layer, bi, 0, 0)),
            pl.BlockSpec(w_o.shape, lambda bi, si: (0, 0)),
        ],
        out_specs=pl.BlockSpec((1, tm, d), lambda bi, si: (bi, si, 0)),
        out_shape=jax.ShapeDtypeStruct(x.shape, F32),
        compiler_params=_compiler_params(
            ("arbitrary", "arbitrary"),
            2 * tile + 2 * _nbytes(w_q.shape, BF16) + _nbytes((m, 2 * d), BF16), 0, 6 * tile),
        name="memory_xattn_sublayer",
    )(x, gpre, gpost, w_q, kv_all, w_o)


def _mlp_kernel(x_ref, gpre_ref, gpost_ref, w1_ref, w2_ref, o_ref):
    d_ff = w1_ref.shape[1]
    x = x_ref[...]
    hn = _rms(x, gpre_ref[...]).astype(BF16)
    acc = jnp.zeros(x.shape, F32)
    for c in range(d_ff // FF_CHUNK):
        cols = slice(c * FF_CHUNK, (c + 1) * FF_CHUNK)
        a = jnp.maximum(jnp.dot(hn, w1_ref[:, cols], preferred_element_type=F32), 0.0)
        acc = acc + jnp.dot((a * a).astype(BF16), w2_ref[cols, :], preferred_element_type=F32)
    o_ref[...] = x + _rms(acc, gpost_ref[...])


def _mlp_layer(x, gpre, gpost, w1, w2):
    b, s, d = x.shape
    tm = TOKEN_TILE
    rows = b * s
    assert rows % tm == 0 and w1.shape[1] % FF_CHUNK == 0
    tile = _nbytes((tm, d), F32)
    weights = _nbytes(w1.shape, BF16) + _nbytes(w2.shape, BF16)
    resident = pl.Buffered(1)
    out = pl.pallas_call(
        _mlp_kernel,
        grid=(rows // tm,),
        in_specs=[
            pl.BlockSpec((tm, d), lambda ri: (ri, 0)),
            pl.BlockSpec((1, d), lambda ri: (0, 0)),
            pl.BlockSpec((1, d), lambda ri: (0, 0)),
            pl.BlockSpec(w1.shape, lambda ri: (0, 0), pipeline_mode=resident),
            pl.BlockSpec(w2.shape, lambda ri: (0, 0), pipeline_mode=resident),
        ],
        out_specs=pl.BlockSpec((tm, d), lambda ri: (ri, 0)),
        out_shape=jax.ShapeDtypeStruct((rows, d), F32),
        compiler_params=_compiler_params(("arbitrary",), 2 * tile, weights, 8 * tile),
        name="mlp_sublayer",
    )(x.reshape(rows, d), gpre, gpost, w1, w2)
    return out.reshape(b, s, d)


def kernel(x, mem, norm_gains, mem_norm, pool_w_in, pool_w_group, pool_scale, moba_w_qkv, moba_w_o,
           xa_w_q, xa_w_kv, xa_w_o, mlp_w1, mlp_w2):
    depth = norm_gains.shape[0]
    s, d = x.shape[1], x.shape[2]
    gain = lambda i, k: norm_gains[i, k][None, :]
    rope = _rope_tables(s)
    kv_all = _mem_kv(mem, mem_norm, xa_w_kv.astype(BF16))
    for i in range(depth):
        j = i // 2
        if i % 2 == 0:
            x = _pool_layer(x, gain(i, 0), gain(i, 1), pool_w_in[j].astype(BF16),
                            pool_w_group[j].astype(BF16), pool_scale[j][None, :])
        else:
            w_qk = moba_w_qkv[j][:, :2 * d].astype(BF16)
            w_vt = moba_w_qkv[j][:, 2 * d:].T.astype(BF16)
            q, k2, vt, kmean = _moba_qkv(x, gain(i, 0), w_qk, w_vt, rope)
            x = _moba_attn(x, q, k2, vt, kmean, moba_w_o[j].astype(BF16), gain(i, 1))
        x = _xattn_layer(x, gain(i, 2), gain(i, 3), xa_w_q[i].astype(BF16), kv_all, i, xa_w_o[i].astype(BF16))
        x = _mlp_layer(x, gain(i, 4), gain(i, 5), mlp_w1[i].astype(BF16), mlp_w2[i].astype(BF16))
    return x
```

```python
import functools

import jax
import jax.numpy as jnp
from jax import lax
from jax.experimental import pallas as pl
from jax.experimental.pallas import tpu as pltpu

F32 = jnp.float32
BF16 = jnp.bfloat16

RMS_EPS = 1e-6
N_NORMS = 6
POOL_WINDOWS = (2, 4, 8, 16)
POOL_GW = 256
POOL_HALO = 16
MOBA_HEADS = 8
HEAD_DIM = 128
ROT_DIM = 32
ROPE_THETA = 500000.0
MOBA_BLOCK = 256
MOBA_TOPK = 3
MEM_HEADS = 4
MEM_HEAD_DIM = 256

TOKEN_TILE = 512
FF_CHUNK = 1024
V7X_VMEM_BYTES = 64 * 1024 * 1024
V7X_VMEM_USABLE = 56 * 1024 * 1024

GATE_ROWS = 16
MASK_BIAS = -1e30
SUM_ROWS = 16
LOG2_E = 1.4426950408889634

NT_DIMS = (((1,), (1,)), ((), ()))
TN_DIMS = (((0,), (0,)), ((), ()))


def _nbytes(shape, dtype):
    n = 1
    for d in shape:
        n *= d
    return n * jnp.dtype(dtype).itemsize


def _compiler_params(semantics, pipelined_bytes, resident_bytes, temp_bytes):
    need = 2 * pipelined_bytes + resident_bytes + temp_bytes
    return pltpu.CompilerParams(
        dimension_semantics=semantics,
        vmem_limit_bytes=int(min(max(need, 16 * 1024 * 1024), V7X_VMEM_USABLE)),
    )


def _layer_block(stack, layer, **kwargs):
    shape = (1,) + tuple(stack.shape[1:])
    zeros = (0,) * (stack.ndim - 1)
    return pl.BlockSpec(shape, lambda *_: (layer,) + zeros, **kwargs)


def _rms(xf, gain):
    ms = jnp.mean(xf * xf, axis=-1, keepdims=True)
    return xf * lax.rsqrt(ms + RMS_EPS) * gain


def _pool_kernel(x_ref, g_ref, win_ref, wg_ref, scale_ref, o_ref, carry_ref, *, tm, gk):
    s = pl.program_id(1)

    @pl.when(s == 0)
    def _():
        carry_ref[...] = jnp.zeros_like(carry_ref)

    x = x_ref[0]
    hn = _rms(x, g_ref[0, gk:gk + 1, :]).astype(BF16)
    u = jnp.dot(hn, win_ref[0].astype(BF16), preferred_element_type=F32)
    ext = jnp.concatenate([carry_ref[...], u], axis=0)
    carry_ref[...] = u[tm - POOL_HALO:, :]

    pos = s * tm + lax.broadcasted_iota(jnp.int32, (tm, 1), 0)
    ys = []
    for g, w in enumerate(POOL_WINDOWS):
        cols = slice(g * POOL_GW, (g + 1) * POOL_GW)
        t = ext[:, cols]
        shift = 1
        while shift < w:
            t = t + pltpu.roll(t, shift, 0)
            shift *= 2
        win_sum = t[POOL_HALO:, :]
        cnt = jnp.minimum(pos + 1, w).astype(F32)
        pooled = win_sum / cnt - u[:, cols]
        ys.append(jnp.dot(pooled.astype(BF16), wg_ref[0, g].astype(BF16), preferred_element_type=F32))
    y = jnp.concatenate(ys, axis=1) * scale_ref[0]
    o_ref[0] = x + _rms(y, g_ref[0, gk + 1:gk + 2, :])


def _pool_layer(x, gains, layer, gk, w_in, w_group, scale, j):
    b, s, d = x.shape
    tm = TOKEN_TILE
    assert s % tm == 0 and tm % 8 == 0 and tm >= POOL_HALO
    tile = _nbytes((tm, d), F32)
    weights = _nbytes(w_in.shape[1:], F32) + _nbytes(w_group.shape[1:], F32)
    return pl.pallas_call(
        functools.partial(_pool_kernel, tm=tm, gk=gk),
        grid=(b, s // tm),
        in_specs=[
            pl.BlockSpec((1, tm, d), lambda bi, si: (bi, si, 0)),
            _layer_block(gains, layer),
            _layer_block(w_in, j),
            _layer_block(w_group, j),
            _layer_block(scale, j),
        ],
        out_specs=pl.BlockSpec((1, tm, d), lambda bi, si: (bi, si, 0)),
        out_shape=jax.ShapeDtypeStruct(x.shape, F32),
        scratch_shapes=[pltpu.VMEM((POOL_HALO, d), F32)],
        compiler_params=_compiler_params(("arbitrary", "arbitrary"), 2 * tile + weights, 0, 8 * tile),
        name="pool_sublayer",
    )(x, gains, w_in, w_group, scale)


def _rope_tables(s):
    half = ROT_DIM // 2
    pos = jnp.arange(s, dtype=F32)
    inv_freq = ROPE_THETA ** (-jnp.arange(0, ROT_DIM, 2, dtype=F32) / ROT_DIM)
    ang = pos[:, None] * inv_freq[None, :]
    cos, sin = jnp.cos(ang), jnp.sin(ang)
    pad = HEAD_DIM - ROT_DIM
    c = jnp.concatenate([cos, cos, jnp.ones((s, pad), F32)], axis=1)
    s_lo = jnp.concatenate([-sin, jnp.zeros((s, HEAD_DIM - half), F32)], axis=1)
    s_hi = jnp.concatenate([jnp.zeros((s, half), F32), sin, jnp.zeros((s, pad), F32)], axis=1)
    k_tab = jnp.stack([c, s_lo, s_hi])
    return jnp.concatenate([k_tab * (LOG2_E * HEAD_DIM ** -0.5), k_tab], axis=0)


def _qkv_kernel(x_ref, g_ref, wq_ref, wk_ref, wv_ref, rope_ref, q_ref, k2_ref, vt_ref, kmean_ref, *, tm, gk):
    s = pl.program_id(1)
    half = ROT_DIM // 2
    blocks_per_tile = tm // MOBA_BLOCK
    hn = _rms(x_ref[0], g_ref[0, gk:gk + 1, :]).astype(BF16)
    for c in range(2 * MOBA_HEADS):
        if c % 2 == 0:
            w_ref = wq_ref if c < MOBA_HEADS else wk_ref
            c0 = c % MOBA_HEADS
            w2 = w_ref[0, :, c0 * HEAD_DIM:(c0 + 2) * HEAD_DIM].astype(BF16)
            y2 = jnp.dot(hn, w2, preferred_element_type=F32)
        y = y2[:, (c % 2) * HEAD_DIM:(c % 2 + 1) * HEAD_DIM]
        t = 0 if c < MOBA_HEADS else 3
        y = (y * rope_ref[t]
             + pltpu.roll(y, HEAD_DIM - half, 1) * rope_ref[t + 1]
             + pltpu.roll(y, half, 1) * rope_ref[t + 2])
        if c < MOBA_HEADS:
            q_ref[0, c] = y.astype(BF16)
        else:
            h = c - MOBA_HEADS
            for r in range(blocks_per_tile):
                row = jnp.mean(y[r * MOBA_BLOCK:(r + 1) * MOBA_BLOCK, :], axis=0, keepdims=True)
                kmean_ref[0, h, pl.ds(s * blocks_per_tile + r, 1), :] = row
            if h % 2 == 0:
                k_even = y
            else:
                k2_ref[0, h // 2] = jnp.concatenate([k_even, y], axis=1).astype(BF16)
    vt = lax.dot_general(wv_ref[0].astype(BF16), hn, (((0,), (1,)), ((), ())), preferred_element_type=F32)
    vt_ref[0] = vt.reshape(MOBA_HEADS, HEAD_DIM, tm).astype(BF16)


def _moba_qkv(x, gains, layer, gk, w_qkv, j, rope):
    b, s, d = x.shape
    tm = TOKEN_TILE
    nb = s // MOBA_BLOCK
    h = MOBA_HEADS
    assert s % tm == 0 and tm % MOBA_BLOCK == 0 and d == h * HEAD_DIM and w_qkv.shape[1:] == (d, 3 * d)
    tile = _nbytes((tm, d), F32)
    out_tile = _nbytes((3 * h, tm, HEAD_DIM), BF16)
    rope_tile = _nbytes((6, tm, HEAD_DIM), F32)
    weights = _nbytes((d, 3 * d), F32)
    w_part = lambda part: pl.BlockSpec((1, d, d), lambda bi, si: (j, 0, part))
    return pl.pallas_call(
        functools.partial(_qkv_kernel, tm=tm, gk=gk),
        grid=(b, s // tm),
        in_specs=[
            pl.BlockSpec((1, tm, d), lambda bi, si: (bi, si, 0)),
            _layer_block(gains, layer),
            w_part(0), w_part(1), w_part(2),
            pl.BlockSpec((6, tm, HEAD_DIM), lambda bi, si: (0, si, 0)),
        ],
        out_specs=[
            pl.BlockSpec((1, h, tm, HEAD_DIM), lambda bi, si: (bi, 0, si, 0)),
            pl.BlockSpec((1, h // 2, tm, 2 * HEAD_DIM), lambda bi, si: (bi, 0, si, 0)),
            pl.BlockSpec((1, h, HEAD_DIM, tm), lambda bi, si: (bi, 0, 0, si)),
            pl.BlockSpec((1, h, nb, HEAD_DIM), lambda bi, si: (bi, 0, 0, 0)),
        ],
        out_shape=[
            jax.ShapeDtypeStruct((b, h, s, HEAD_DIM), BF16),
            jax.ShapeDtypeStruct((b, h // 2, s, 2 * HEAD_DIM), BF16),
            jax.ShapeDtypeStruct((b, h, HEAD_DIM, s), BF16),
            jax.ShapeDtypeStruct((b, h, nb, HEAD_DIM), F32),
        ],
        compiler_params=_compiler_params(
            ("arbitrary", "arbitrary"), tile + out_tile + rope_tile + weights, 0, 4 * tile),
        name="moba_qkv",
    )(x, gains, w_qkv, w_qkv, w_qkv, rope)


def _attn_kernel(q_ref, k2_ref, vt_ref, km_ref, x_ref, wo_ref, g_ref, o_ref, ot_ref, *, gk):
    i = pl.program_id(1)
    blk = MOBA_BLOCK
    hd = HEAD_DIM
    nb = k2_ref.shape[2] // blk
    d = x_ref.shape[2]
    nbp = GATE_ROWS

    key_id = lax.broadcasted_iota(jnp.int32, (blk, 2 * blk), 0)
    qry_id = lax.broadcasted_iota(jnp.int32, (blk, 2 * blk), 1) & (blk - 1)
    causal = key_id <= qry_id
    blk_id = lax.broadcasted_iota(jnp.int32, (nbp, blk), 0)
    zeros_q = jnp.zeros((blk, hd), BF16)
    ones = jnp.ones((SUM_ROWS, blk), BF16)

    for ii in range(nb):
        @pl.when(i == ii)
        def _(ii=ii):
            gated = ii > MOBA_TOPK

            def block_bias(q, h):
                km = km_ref[0, h]
                km = jnp.concatenate([km, jnp.zeros((nbp - nb, hd), F32)], axis=0)
                km_hi = km.astype(BF16)
                km_lo = (km - km_hi.astype(F32)).astype(BF16)
                gate = (lax.dot_general(km_hi, q, NT_DIMS, preferred_element_type=F32)
                        + lax.dot_general(km_lo, q, NT_DIMS, preferred_element_type=F32))
                rank = jnp.zeros((nbp, blk), F32)
                for jp in range(ii):
                    gj = gate[jp:jp + 1, :]
                    beats = (gj > gate) | ((gj == gate) & (blk_id > jp))
                    rank = rank + beats.astype(F32)
                return jnp.where(rank >= float(MOBA_TOPK), MASK_BIAS, 0.0)

            def pair(g, carry):
                q0 = q_ref[0, 2 * g]
                q1 = q_ref[0, 2 * g + 1]
                qd = jnp.concatenate([jnp.concatenate([q0, zeros_q], axis=1),
                                      jnp.concatenate([zeros_q, q1], axis=1)], axis=0)
                if gated:
                    bias = jnp.concatenate([block_bias(q0, 2 * g), block_bias(q1, 2 * g + 1)], axis=1)
                m = None
                acc = [None, None]
                order = [ii] + list(range(ii))

                def scores(j):
                    sc = lax.dot_general(k2_ref[0, g, j * blk:(j + 1) * blk, :], qd, NT_DIMS,
                                         preferred_element_type=F32)
                    if j == ii:
                        sc = jnp.where(causal, sc, -jnp.inf)
                    bm = jnp.max(sc, axis=0, keepdims=True)
                    if gated and j < ii:
                        bm = bm + bias[j:j + 1, :]
                    return sc, bm

                def accumulate(j, p, alpha):
                    for hh in range(2):
                        cols = slice(hh * blk, (hh + 1) * blk)
                        vt1 = jnp.concatenate([vt_ref[0, 2 * g + hh, :, j * blk:(j + 1) * blk], ones], axis=0)
                        pv = jnp.dot(vt1, p[:, cols], preferred_element_type=F32)
                        acc[hh] = pv if alpha is None else acc[hh] * alpha[:, cols] + pv

                n_blk = len(order)
                sc_q = [scores(order[0])]
                if n_blk > 1:
                    sc_q.append(scores(order[1]))
                pending = None
                for t, j in enumerate(order):
                    sc, bm = sc_q.pop(0)
                    if t + 2 < n_blk:
                        sc_q.append(scores(order[t + 2]))
                    if pending is not None:
                        accumulate(*pending)
                    masked = gated and j < ii
                    m_new = bm if m is None else jnp.maximum(m, bm)
                    shift = bias[j:j + 1, :] - m_new if masked else -m_new
                    p = jnp.exp2(sc + shift).astype(BF16)
                    alpha = None if m is None else jnp.exp2(m - m_new)
                    pending = (j, p, alpha)
                    m = m_new
                accumulate(*pending)
                for hh in range(2):
                    ot_ref[2 * g + hh] = (acc[hh][:hd, :] / acc[hh][hd:hd + 1, :]).astype(BF16)
                return carry

            lax.fori_loop(0, MOBA_HEADS // 2, pair, 0, unroll=2)

    o_t = ot_ref[...].reshape(d, blk)
    y = lax.dot_general(o_t, wo_ref[0].astype(BF16), TN_DIMS, preferred_element_type=F32)
    o_ref[0] = x_ref[0] + _rms(y, g_ref[0, gk:gk + 1, :])


def _moba_attn(x, q, k2, vt, kmean, w_o, j, gains, layer, gk):
    b, s, d = x.shape
    blk = MOBA_BLOCK
    nb = s // blk
    h = MOBA_HEADS
    assert nb <= GATE_ROWS and blk == 256 and h % 2 == 0
    q_tile = _nbytes((h, blk, HEAD_DIM), BF16)
    kv_tile = _nbytes((h, s, HEAD_DIM), BF16)
    x_tile = _nbytes((blk, d), F32)
    return pl.pallas_call(
        functools.partial(_attn_kernel, gk=gk),
        grid=(b, nb),
        in_specs=[
            pl.BlockSpec((1, h, blk, HEAD_DIM), lambda bi, qi: (bi, 0, qi, 0)),
            pl.BlockSpec((1, h // 2, s, 2 * HEAD_DIM), lambda bi, qi: (bi, 0, 0, 0)),
            pl.BlockSpec((1, h, HEAD_DIM, s), lambda bi, qi: (bi, 0, 0, 0)),
            pl.BlockSpec((1, h, nb, HEAD_DIM), lambda bi, qi: (bi, 0, 0, 0)),
            pl.BlockSpec((1, blk, d), lambda bi, qi: (bi, qi, 0)),
            _layer_block(w_o, j),
            _layer_block(gains, layer),
        ],
        out_specs=pl.BlockSpec((1, blk, d), lambda bi, qi: (bi, qi, 0)),
        out_shape=jax.ShapeDtypeStruct(x.shape, F32),
        scratch_shapes=[pltpu.VMEM((h, HEAD_DIM, blk), BF16)],
        compiler_params=_compiler_params(
            ("arbitrary", "arbitrary"), q_tile + 2 * kv_tile + 2 * x_tile + _nbytes((d, d), F32),
            x_tile, 8 * _nbytes((blk, s), F32)),
        name="moba_attention",
    )(q, k2, vt, kmean, x, w_o, gains)


def _memkv_kernel(mem_ref, g_ref, w_ref, o_ref):
    mn = _rms(mem_ref[0], g_ref[0]).astype(BF16)
    o_ref[0, 0] = jnp.dot(mn, w_ref[0].astype(BF16), preferred_element_type=F32).astype(BF16)


def _mem_kv(mem, mem_norm, w_kv):
    depth, d, d2 = w_kv.shape
    b, m, _ = mem.shape
    return pl.pallas_call(
        _memkv_kernel,
        grid=(depth, b),
        in_specs=[
            pl.BlockSpec((1, m, d), lambda li, bi: (bi, 0, 0)),
            pl.BlockSpec((1, 1, d), lambda li, bi: (li, 0, 0)),
            pl.BlockSpec((1, d, d2), lambda li, bi: (li, 0, 0)),
        ],
        out_specs=pl.BlockSpec((1, 1, m, d2), lambda li, bi: (li, bi, 0, 0)),
        out_shape=jax.ShapeDtypeStruct((depth, b, m, d2), BF16),
        compiler_params=_compiler_params(
            ("arbitrary", "arbitrary"),
            _nbytes((m, d), F32) + _nbytes((d, d2), F32) + _nbytes((m, d2), BF16), 0, 4 * _nbytes((m, d2), F32)),
        name="memory_kv",
    )(mem, mem_norm.reshape(depth, 1, d), w_kv)


def _xattn_kernel(x_ref, g_ref, wq_ref, kv_ref, wo_ref, o_ref, *, gk):
    d = x_ref.shape[2]
    hd = MEM_HEAD_DIM
    x = x_ref[0]
    hn = _rms(x, g_ref[0, gk:gk + 1, :]).astype(BF16)
    q = (jnp.dot(hn, wq_ref[0].astype(BF16), preferred_element_type=F32) * (hd ** -0.5)).astype(BF16)
    outs = []
    for h in range(MEM_HEADS):
        qh = q[:, h * hd:(h + 1) * hd]
        kh = kv_ref[0, 0, :, h * hd:(h + 1) * hd]
        vh = kv_ref[0, 0, :, d + h * hd:d + (h + 1) * hd]
        sc = lax.dot_general(qh, kh, NT_DIMS, preferred_element_type=F32)
        m = jnp.max(sc, axis=1, keepdims=True)
        p = jnp.exp(sc - m)
        l = jnp.sum(p, axis=1, keepdims=True)
        outs.append(jnp.dot(p.astype(BF16), vh, preferred_element_type=F32) / l)
    o = jnp.concatenate(outs, axis=1).astype(BF16)
    y = jnp.dot(o, wo_ref[0].astype(BF16), preferred_element_type=F32)
    o_ref[0] = x + _rms(y, g_ref[0, gk + 1:gk + 2, :])


def _xattn_layer(x, gains, layer, gk, w_q, kv_all, w_o):
    b, s, d = x.shape
    tm = TOKEN_TILE
    m = kv_all.shape[2]
    assert s % tm == 0 and d == MEM_HEADS * MEM_HEAD_DIM
    tile = _nbytes((tm, d), F32)
    return pl.pallas_call(
        functools.partial(_xattn_kernel, gk=gk),
        grid=(b, s // tm),
        in_specs=[
            pl.BlockSpec((1, tm, d), lambda bi, si: (bi, si, 0)),
            _layer_block(gains, layer),
            _layer_block(w_q, layer),
            pl.BlockSpec((1, 1, m, 2 * d), lambda bi, si: (layer, bi, 0, 0)),
            _layer_block(w_o, layer),
        ],
        out_specs=pl.BlockSpec((1, tm, d), lambda bi, si: (bi, si, 0)),
        out_shape=jax.ShapeDtypeStruct(x.shape, F32),
        compiler_params=_compiler_params(
            ("arbitrary", "arbitrary"),
            2 * tile + 2 * _nbytes((d, d), F32) + _nbytes((m, 2 * d), BF16), 0, 6 * tile),
        name="memory_xattn_sublayer",
    )(x, gains, w_q, kv_all, w_o)


def _mlp_kernel(x_ref, g_ref, w1_ref, w2_ref, o_ref, *, gk):
    d_ff = w1_ref.shape[2]
    x = x_ref[...]
    hn = _rms(x, g_ref[0, gk:gk + 1, :]).astype(BF16)
    acc = jnp.zeros(x.shape, F32)
    for c in range(d_ff // FF_CHUNK):
        cols = slice(c * FF_CHUNK, (c + 1) * FF_CHUNK)
        a = jnp.maximum(jnp.dot(hn, w1_ref[0, :, cols].astype(BF16), preferred_element_type=F32), 0.0)
        acc = acc + jnp.dot((a * a).astype(BF16), w2_ref[0, cols, :].astype(BF16), preferred_element_type=F32)
    o_ref[...] = x + _rms(acc, g_ref[0, gk + 1:gk + 2, :])


def _mlp_layer(x, gains, layer, gk, w1, w2):
    b, s, d = x.shape
    tm = TOKEN_TILE
    rows = b * s
    d_ff = w1.shape[2]
    assert rows % tm == 0 and d_ff % FF_CHUNK == 0
    tile = _nbytes((tm, d), F32)
    weights = 2 * _nbytes((d, d_ff), F32)
    resident = pl.Buffered(1)
    out = pl.pallas_call(
        functools.partial(_mlp_kernel, gk=gk),
        grid=(rows // tm,),
        in_specs=[
            pl.BlockSpec((tm, d), lambda ri: (ri, 0)),
            _layer_block(gains, layer),
            _layer_block(w1, layer, pipeline_mode=resident),
            _layer_block(w2, layer, pipeline_mode=resident),
        ],
        out_specs=pl.BlockSpec((tm, d), lambda ri: (ri, 0)),
        out_shape=jax.ShapeDtypeStruct((rows, d), F32),
        compiler_params=_compiler_params(("arbitrary",), 2 * tile, weights, 6 * tile),
        name="mlp_sublayer",
    )(x.reshape(rows, d), gains, w1, w2)
    return out.reshape(b, s, d)


def kernel(x, mem, norm_gains, mem_norm, pool_w_in, pool_w_group, pool_scale, moba_w_qkv, moba_w_o,
           xa_w_q, xa_w_kv, xa_w_o, mlp_w1, mlp_w2):
    depth = norm_gains.shape[0]
    s, d = x.shape[1], x.shape[2]
    assert norm_gains.shape[1] == N_NORMS
    rope = _rope_tables(s)
    scale = pool_scale.reshape(pool_scale.shape[0], 1, d)
    kv_all = _mem_kv(mem, mem_norm, xa_w_kv)
    for i in range(depth):
        j = i // 2
        if i % 2 == 0:
            x = _pool_layer(x, norm_gains, i, 0, pool_w_in, pool_w_group, scale, j)
        else:
            q, k2, vt, kmean = _moba_qkv(x, norm_gains, i, 0, moba_w_qkv, j, rope)
            x = _moba_attn(x, q, k2, vt, kmean, moba_w_o, j, norm_gains, i, 1)
        x = _xattn_layer(x, norm_gains, i, 2, xa_w_q, kv_all, xa_w_o)
        x = _mlp_layer(x, norm_gains, i, 4, mlp_w1, mlp_w2)
    return x
```

```python
import functools

import jax
import jax.numpy as jnp
from jax import lax
from jax.experimental import pallas as pl
from jax.experimental.pallas import tpu as pltpu

F32 = jnp.float32
BF16 = jnp.bfloat16

RMS_EPS = 1e-6
N_NORMS = 6
POOL_WINDOWS = (2, 4, 8, 16)
POOL_GW = 256
POOL_HALO = 16
MOBA_HEADS = 8
HEAD_DIM = 128
ROT_DIM = 32
ROPE_THETA = 500000.0
MOBA_BLOCK = 256
MOBA_TOPK = 3
MEM_HEADS = 4
MEM_HEAD_DIM = 256

TOKEN_TILE = 512
XATTN_TILE = 1024
ATTN_QBLOCKS = 2
SUB_ROWS = 256
FF_CHUNK = 1024
V7X_VMEM_BYTES = 64 * 1024 * 1024
V7X_VMEM_USABLE = 56 * 1024 * 1024

GATE_ROWS = 16
MASK_BIAS = -1e30
SUM_ROWS = 16
LOG2_E = 1.4426950408889634

NT_DIMS = (((1,), (1,)), ((), ()))
TN_DIMS = (((0,), (0,)), ((), ()))


def _nbytes(shape, dtype):
    n = 1
    for d in shape:
        n *= d
    return n * jnp.dtype(dtype).itemsize


def _compiler_params(semantics, pipelined_bytes, resident_bytes, temp_bytes):
    need = 2 * pipelined_bytes + resident_bytes + temp_bytes
    return pltpu.CompilerParams(
        dimension_semantics=semantics,
        vmem_limit_bytes=int(min(max(need, 16 * 1024 * 1024), V7X_VMEM_USABLE)),
    )


def _layer_block(stack, layer, **kwargs):
    shape = (1,) + tuple(stack.shape[1:])
    zeros = (0,) * (stack.ndim - 1)
    return pl.BlockSpec(shape, lambda *_: (layer,) + zeros, **kwargs)


def _sub_tiles(rows):
    assert rows % SUB_ROWS == 0
    return [slice(r, r + SUB_ROWS) for r in range(0, rows, SUB_ROWS)]


def _rms(xf, gain):
    ms = jnp.mean(xf * xf, axis=-1, keepdims=True)
    return xf * lax.rsqrt(ms + RMS_EPS) * gain


def _pool_kernel(x_ref, g_ref, win_ref, wg_ref, scale_ref, o_ref, carry_ref, *, tm, gk):
    s = pl.program_id(1)

    @pl.when(s == 0)
    def _():
        carry_ref[...] = jnp.zeros_like(carry_ref)

    x = x_ref[0]
    hn = _rms(x, g_ref[0, gk:gk + 1, :]).astype(BF16)
    u = jnp.dot(hn, win_ref[0].astype(BF16), preferred_element_type=F32)
    ext = jnp.concatenate([carry_ref[...], u], axis=0)
    carry_ref[...] = u[tm - POOL_HALO:, :]

    pos = s * tm + lax.broadcasted_iota(jnp.int32, (tm, 1), 0)
    ys = []
    for g, w in enumerate(POOL_WINDOWS):
        cols = slice(g * POOL_GW, (g + 1) * POOL_GW)
        t = ext[:, cols]
        shift = 1
        while shift < w:
            t = t + pltpu.roll(t, shift, 0)
            shift *= 2
        win_sum = t[POOL_HALO:, :]
        cnt = jnp.minimum(pos + 1, w).astype(F32)
        pooled = win_sum / cnt - u[:, cols]
        ys.append(jnp.dot(pooled.astype(BF16), wg_ref[0, g].astype(BF16), preferred_element_type=F32))
    y = jnp.concatenate(ys, axis=1) * scale_ref[0]
    o_ref[0] = x + _rms(y, g_ref[0, gk + 1:gk + 2, :])


def _pool_layer(x, gains, layer, gk, w_in, w_group, scale, j):
    b, s, d = x.shape
    tm = TOKEN_TILE
    assert s % tm == 0 and tm % 8 == 0 and tm >= POOL_HALO
    tile = _nbytes((tm, d), F32)
    weights = _nbytes(w_in.shape[1:], F32) + _nbytes(w_group.shape[1:], F32)
    return pl.pallas_call(
        functools.partial(_pool_kernel, tm=tm, gk=gk),
        grid=(b, s // tm),
        in_specs=[
            pl.BlockSpec((1, tm, d), lambda bi, si: (bi, si, 0)),
            _layer_block(gains, layer),
            _layer_block(w_in, j),
            _layer_block(w_group, j),
            _layer_block(scale, j),
        ],
        out_specs=pl.BlockSpec((1, tm, d), lambda bi, si: (bi, si, 0)),
        out_shape=jax.ShapeDtypeStruct(x.shape, F32),
        scratch_shapes=[pltpu.VMEM((POOL_HALO, d), F32)],
        compiler_params=_compiler_params(("arbitrary", "arbitrary"), 2 * tile + weights, 0, 8 * tile),
        name="pool_sublayer",
    )(x, gains, w_in, w_group, scale)


def _rope_tables(s):
    half = ROT_DIM // 2
    pos = jnp.arange(s, dtype=F32)
    inv_freq = ROPE_THETA ** (-jnp.arange(0, ROT_DIM, 2, dtype=F32) / ROT_DIM)
    ang = pos[:, None] * inv_freq[None, :]
    cos, sin = jnp.cos(ang), jnp.sin(ang)
    pad = HEAD_DIM - ROT_DIM
    c = jnp.concatenate([cos, cos, jnp.ones((s, pad), F32)], axis=1)
    s_lo = jnp.concatenate([-sin, jnp.zeros((s, HEAD_DIM - half), F32)], axis=1)
    s_hi = jnp.concatenate([jnp.zeros((s, half), F32), sin, jnp.zeros((s, pad), F32)], axis=1)
    k_tab = jnp.stack([c, s_lo, s_hi])
    return jnp.concatenate([k_tab * (LOG2_E * HEAD_DIM ** -0.5), k_tab], axis=0)


def _qkv_kernel(x_ref, g_ref, wq_ref, wk_ref, wv_ref, rope_ref, q_ref, k2_ref, vt_ref, kmean_ref, *, tm, gk):
    s = pl.program_id(1)
    half = ROT_DIM // 2
    blocks_per_tile = tm // MOBA_BLOCK
    hn = _rms(x_ref[0], g_ref[0, gk:gk + 1, :]).astype(BF16)
    for c in range(2 * MOBA_HEADS):
        if c % 2 == 0:
            w_ref = wq_ref if c < MOBA_HEADS else wk_ref
            c0 = c % MOBA_HEADS
            w2 = w_ref[0, :, c0 * HEAD_DIM:(c0 + 2) * HEAD_DIM].astype(BF16)
            y2 = jnp.dot(hn, w2, preferred_element_type=F32)
        y = y2[:, (c % 2) * HEAD_DIM:(c % 2 + 1) * HEAD_DIM]
        t = 0 if c < MOBA_HEADS else 3
        y = (y * rope_ref[t]
             + pltpu.roll(y, HEAD_DIM - half, 1) * rope_ref[t + 1]
             + pltpu.roll(y, half, 1) * rope_ref[t + 2])
        if c < MOBA_HEADS:
            q_ref[0, c] = y.astype(BF16)
        else:
            h = c - MOBA_HEADS
            for r in range(blocks_per_tile):
                row = jnp.mean(y[r * MOBA_BLOCK:(r + 1) * MOBA_BLOCK, :], axis=0, keepdims=True)
                kmean_ref[0, h, pl.ds(s * blocks_per_tile + r, 1), :] = row
            if h % 2 == 0:
                k_even = y
            else:
                k2_ref[0, h // 2] = jnp.concatenate([k_even, y], axis=1).astype(BF16)
    vt = lax.dot_general(wv_ref[0].astype(BF16), hn, (((0,), (1,)), ((), ())), preferred_element_type=F32)
    vt_ref[0] = vt.reshape(MOBA_HEADS, HEAD_DIM, tm).astype(BF16)


def _moba_qkv(x, gains, layer, gk, w_qkv, j, rope):
    b, s, d = x.shape
    tm = TOKEN_TILE
    nb = s // MOBA_BLOCK
    h = MOBA_HEADS
    assert s % tm == 0 and tm % MOBA_BLOCK == 0 and d == h * HEAD_DIM and w_qkv.shape[1:] == (d, 3 * d)
    tile = _nbytes((tm, d), F32)
    out_tile = _nbytes((3 * h, tm, HEAD_DIM), BF16)
    rope_tile = _nbytes((6, tm, HEAD_DIM), F32)
    weights = _nbytes((d, 3 * d), F32)
    w_part = lambda part: pl.BlockSpec((1, d, d), lambda bi, si: (j, 0, part))
    return pl.pallas_call(
        functools.partial(_qkv_kernel, tm=tm, gk=gk),
        grid=(b, s // tm),
        in_specs=[
            pl.BlockSpec((1, tm, d), lambda bi, si: (bi, si, 0)),
            _layer_block(gains, layer),
            w_part(0), w_part(1), w_part(2),
            pl.BlockSpec((6, tm, HEAD_DIM), lambda bi, si: (0, si, 0)),
        ],
        out_specs=[
            pl.BlockSpec((1, h, tm, HEAD_DIM), lambda bi, si: (bi, 0, si, 0)),
            pl.BlockSpec((1, h // 2, tm, 2 * HEAD_DIM), lambda bi, si: (bi, 0, si, 0)),
            pl.BlockSpec((1, h, HEAD_DIM, tm), lambda bi, si: (bi, 0, 0, si)),
            pl.BlockSpec((1, h, nb, HEAD_DIM), lambda bi, si: (bi, 0, 0, 0)),
        ],
        out_shape=[
            jax.ShapeDtypeStruct((b, h, s, HEAD_DIM), BF16),
            jax.ShapeDtypeStruct((b, h // 2, s, 2 * HEAD_DIM), BF16),
            jax.ShapeDtypeStruct((b, h, HEAD_DIM, s), BF16),
            jax.ShapeDtypeStruct((b, h, nb, HEAD_DIM), F32),
        ],
        compiler_params=_compiler_params(
            ("arbitrary", "arbitrary"), tile + out_tile + rope_tile + weights, 0, 4 * tile),
        name="moba_qkv",
    )(x, gains, w_qkv, w_qkv, w_qkv, rope)


def _attn_query_block(q_ref, k2_ref, vt_ref, km_ref, ot_ref, ii, qb):
    blk = MOBA_BLOCK
    hd = HEAD_DIM
    nb = k2_ref.shape[2] // blk
    nbp = GATE_ROWS
    gated = ii > MOBA_TOPK
    q_rows = slice(qb * blk, (qb + 1) * blk)

    key_id = lax.broadcasted_iota(jnp.int32, (blk, 2 * blk), 0)
    qry_id = lax.broadcasted_iota(jnp.int32, (blk, 2 * blk), 1) & (blk - 1)
    causal = key_id <= qry_id
    blk_id = lax.broadcasted_iota(jnp.int32, (nbp, blk), 0)
    zeros_q = jnp.zeros((blk, hd), BF16)
    ones = jnp.ones((SUM_ROWS, blk), BF16)

    def block_bias(q, h):
        km = km_ref[0, h]
        km = jnp.concatenate([km, jnp.zeros((nbp - nb, hd), F32)], axis=0)
        km_hi = km.astype(BF16)
        km_lo = (km - km_hi.astype(F32)).astype(BF16)
        gate = (lax.dot_general(km_hi, q, NT_DIMS, preferred_element_type=F32)
                + lax.dot_general(km_lo, q, NT_DIMS, preferred_element_type=F32))
        rank = jnp.zeros((nbp, blk), F32)
        for jp in range(ii):
            gj = gate[jp:jp + 1, :]
            beats = (gj > gate) | ((gj == gate) & (blk_id > jp))
            rank = rank + beats.astype(F32)
        return jnp.where(rank >= float(MOBA_TOPK), MASK_BIAS, 0.0)

    def pair(g, carry):
        q0 = q_ref[0, 2 * g, q_rows, :]
        q1 = q_ref[0, 2 * g + 1, q_rows, :]
        qd = jnp.concatenate([jnp.concatenate([q0, zeros_q], axis=1),
                              jnp.concatenate([zeros_q, q1], axis=1)], axis=0)
        if gated:
            bias = jnp.concatenate([block_bias(q0, 2 * g), block_bias(q1, 2 * g + 1)], axis=1)
        m = None
        acc = [None, None]
        order = [ii] + list(range(ii))

        def scores(j):
            sc = lax.dot_general(k2_ref[0, g, j * blk:(j + 1) * blk, :], qd, NT_DIMS,
                                 preferred_element_type=F32)
            if j == ii:
                sc = jnp.where(causal, sc, -jnp.inf)
            bm = jnp.max(sc, axis=0, keepdims=True)
            if gated and j < ii:
                bm = bm + bias[j:j + 1, :]
            return sc, bm

        def accumulate(j, p, alpha):
            for hh in range(2):
                cols = slice(hh * blk, (hh + 1) * blk)
                vt1 = jnp.concatenate([vt_ref[0, 2 * g + hh, :, j * blk:(j + 1) * blk], ones], axis=0)
                pv = jnp.dot(vt1, p[:, cols], preferred_element_type=F32)
                acc[hh] = pv if alpha is None else acc[hh] * alpha[:, cols] + pv

        n_blk = len(order)
        sc_q = [scores(order[0])]
        if n_blk > 1:
            sc_q.append(scores(order[1]))
        pending = None
        for t, j in enumerate(order):
            sc, bm = sc_q.pop(0)
            if t + 2 < n_blk:
                sc_q.append(scores(order[t + 2]))
            if pending is not None:
                accumulate(*pending)
            masked = gated and j < ii
            m_new = bm if m is None else jnp.maximum(m, bm)
            shift = bias[j:j + 1, :] - m_new if masked else -m_new
            p = jnp.exp2(sc + shift).astype(BF16)
            alpha = None if m is None else jnp.exp2(m - m_new)
            pending = (j, p, alpha)
            m = m_new
        accumulate(*pending)
        for hh in range(2):
            ot_ref[qb, 2 * g + hh] = (acc[hh][:hd, :] / acc[hh][hd:hd + 1, :]).astype(BF16)
        return carry

    lax.fori_loop(0, MOBA_HEADS // 2, pair, 0, unroll=2)


def _attn_kernel(q_ref, k2_ref, vt_ref, km_ref, x_ref, wo_ref, g_ref, o_ref, ot_ref, *, gk):
    i = pl.program_id(1)
    blk = MOBA_BLOCK
    nb = k2_ref.shape[2] // blk
    d = x_ref.shape[2]

    for step in range(nb // ATTN_QBLOCKS):
        for qb in range(ATTN_QBLOCKS):
            pl.when(i == step)(functools.partial(
                _attn_query_block, q_ref, k2_ref, vt_ref, km_ref, ot_ref, step * ATTN_QBLOCKS + qb, qb))

    o_t = jnp.concatenate([ot_ref[qb].reshape(d, blk) for qb in range(ATTN_QBLOCKS)], axis=1)
    y = lax.dot_general(o_t, wo_ref[0].astype(BF16), TN_DIMS, preferred_element_type=F32)
    o_ref[0] = x_ref[0] + _rms(y, g_ref[0, gk:gk + 1, :])


def _moba_attn(x, q, k2, vt, kmean, w_o, j, gains, layer, gk):
    b, s, d = x.shape
    blk = MOBA_BLOCK
    nb = s // blk
    h = MOBA_HEADS
    rows = ATTN_QBLOCKS * blk
    assert nb <= GATE_ROWS and blk == 256 and h % 2 == 0 and nb % ATTN_QBLOCKS == 0
    q_tile = _nbytes((h, rows, HEAD_DIM), BF16)
    kv_tile = _nbytes((h, s, HEAD_DIM), BF16)
    x_tile = _nbytes((rows, d), F32)
    return pl.pallas_call(
        functools.partial(_attn_kernel, gk=gk),
        grid=(b, nb // ATTN_QBLOCKS),
        in_specs=[
            pl.BlockSpec((1, h, rows, HEAD_DIM), lambda bi, qi: (bi, 0, qi, 0)),
            pl.BlockSpec((1, h // 2, s, 2 * HEAD_DIM), lambda bi, qi: (bi, 0, 0, 0)),
            pl.BlockSpec((1, h, HEAD_DIM, s), lambda bi, qi: (bi, 0, 0, 0)),
            pl.BlockSpec((1, h, nb, HEAD_DIM), lambda bi, qi: (bi, 0, 0, 0)),
            pl.BlockSpec((1, rows, d), lambda bi, qi: (bi, qi, 0)),
            _layer_block(w_o, j),
            _layer_block(gains, layer),
        ],
        out_specs=pl.BlockSpec((1, rows, d), lambda bi, qi: (bi, qi, 0)),
        out_shape=jax.ShapeDtypeStruct(x.shape, F32),
        scratch_shapes=[pltpu.VMEM((ATTN_QBLOCKS, h, HEAD_DIM, blk), BF16)],
        compiler_params=_compiler_params(
            ("arbitrary", "arbitrary"), q_tile + 2 * kv_tile + 2 * x_tile + _nbytes((d, d), F32),
            x_tile, 8 * _nbytes((blk, s), F32)),
        name="moba_attention",
    )(q, k2, vt, kmean, x, w_o, gains)


def _memkv_kernel(mem_ref, g_ref, w_ref, o_ref):
    mn = _rms(mem_ref[0], g_ref[0]).astype(BF16)
    o_ref[0, 0] = jnp.dot(mn, w_ref[0].astype(BF16), preferred_element_type=F32).astype(BF16)


def _mem_kv(mem, mem_norm, w_kv):
    depth, d, d2 = w_kv.shape
    b, m, _ = mem.shape
    return pl.pallas_call(
        _memkv_kernel,
        grid=(depth, b),
        in_specs=[
            pl.BlockSpec((1, m, d), lambda li, bi: (bi, 0, 0)),
            pl.BlockSpec((1, 1, d), lambda li, bi: (li, 0, 0)),
            pl.BlockSpec((1, d, d2), lambda li, bi: (li, 0, 0)),
        ],
        out_specs=pl.BlockSpec((1, 1, m, d2), lambda li, bi: (li, bi, 0, 0)),
        out_shape=jax.ShapeDtypeStruct((depth, b, m, d2), BF16),
        compiler_params=_compiler_params(
            ("arbitrary", "arbitrary"),
            _nbytes((m, d), F32) + _nbytes((d, d2), F32) + _nbytes((m, d2), BF16), 0, 4 * _nbytes((m, d2), F32)),
        name="memory_kv",
    )(mem, mem_norm.reshape(depth, 1, d), w_kv)


def _xattn_kernel(x_ref, g_ref, wq_ref, kv_ref, wo_ref, o_ref, *, gk):
    d = x_ref.shape[2]
    hd = MEM_HEAD_DIM
    subs = _sub_tiles(x_ref.shape[1])
    heads = range(MEM_HEADS)
    xs = [x_ref[0, rows, :] for rows in subs]
    hn = [_rms(x, g_ref[0, gk:gk + 1, :]).astype(BF16) for x in xs]
    wq = wq_ref[0].astype(BF16)
    q_scale = LOG2_E * hd ** -0.5
    q = [(jnp.dot(h, wq, preferred_element_type=F32) * q_scale).astype(BF16) for h in hn]
    sc = [[lax.dot_general(qs[:, h * hd:(h + 1) * hd], kv_ref[0, 0, :, h * hd:(h + 1) * hd], NT_DIMS,
                           preferred_element_type=F32) for h in heads] for qs in q]
    p, inv_l = [], []
    for sc_s in sc:
        e = [jnp.exp2(t - jnp.max(t, axis=1, keepdims=True)) for t in sc_s]
        inv_l.append([1.0 / jnp.sum(t, axis=1, keepdims=True) for t in e])
        p.append([t.astype(BF16) for t in e])
    wo = wo_ref[0].astype(BF16)
    for rows, x, p_s, inv_s in zip(subs, xs, p, inv_l):
        outs = [jnp.dot(p_s[h], kv_ref[0, 0, :, d + h * hd:d + (h + 1) * hd], preferred_element_type=F32) * inv_s[h]
                for h in heads]
        o = jnp.concatenate(outs, axis=1).astype(BF16)
        y = jnp.dot(o, wo, preferred_element_type=F32)
        o_ref[0, rows, :] = x + _rms(y, g_ref[0, gk + 1:gk + 2, :])


def _xattn_layer(x, gains, layer, gk, w_q, kv_all, w_o):
    b, s, d = x.shape
    tm = XATTN_TILE
    m = kv_all.shape[2]
    assert s % tm == 0 and d == MEM_HEADS * MEM_HEAD_DIM
    tile = _nbytes((tm, d), F32)
    return pl.pallas_call(
        functools.partial(_xattn_kernel, gk=gk),
        grid=(b, s // tm),
        in_specs=[
            pl.BlockSpec((1, tm, d), lambda bi, si: (bi, si, 0)),
            _layer_block(gains, layer),
            _layer_block(w_q, layer),
            pl.BlockSpec((1, 1, m, 2 * d), lambda bi, si: (layer, bi, 0, 0)),
            _layer_block(w_o, layer),
        ],
        out_specs=pl.BlockSpec((1, tm, d), lambda bi, si: (bi, si, 0)),
        out_shape=jax.ShapeDtypeStruct(x.shape, F32),
        compiler_params=_compiler_params(
            ("arbitrary", "arbitrary"),
            2 * tile + 2 * _nbytes((d, d), F32) + _nbytes((m, 2 * d), BF16), 0, 6 * tile),
        name="memory_xattn_sublayer",
    )(x, gains, w_q, kv_all, w_o)


def _mlp_kernel(x_ref, g_ref, w1_ref, w2_ref, o_ref, *, gk):
    d_ff = w1_ref.shape[2]
    subs = _sub_tiles(x_ref.shape[0])
    xs = [x_ref[rows, :] for rows in subs]
    hn = [_rms(x, g_ref[0, gk:gk + 1, :]).astype(BF16) for x in xs]
    acc = [jnp.zeros(x.shape, F32) for x in xs]
    for c in range(d_ff // FF_CHUNK):
        cols = slice(c * FF_CHUNK, (c + 1) * FF_CHUNK)
        w1c = w1_ref[0, :, cols].astype(BF16)
        w2c = w2_ref[0, cols, :].astype(BF16)
        a = [jnp.maximum(jnp.dot(h, w1c, preferred_element_type=F32), 0.0) for h in hn]
        acc = [t + jnp.dot((u * u).astype(BF16), w2c, preferred_element_type=F32) for t, u in zip(acc, a)]
    for rows, x, t in zip(subs, xs, acc):
        o_ref[rows, :] = x + _rms(t, g_ref[0, gk + 1:gk + 2, :])


def _mlp_layer(x, gains, layer, gk, w1, w2):
    b, s, d = x.shape
    tm = TOKEN_TILE
    rows = b * s
    d_ff = w1.shape[2]
    assert rows % tm == 0 and d_ff % FF_CHUNK == 0
    tile = _nbytes((tm, d), F32)
    weights = 2 * _nbytes((d, d_ff), F32)
    resident = pl.Buffered(1)
    out = pl.pallas_call(
        functools.partial(_mlp_kernel, gk=gk),
        grid=(rows // tm,),
        in_specs=[
            pl.BlockSpec((tm, d), lambda ri: (ri, 0)),
            _layer_block(gains, layer),
            _layer_block(w1, layer, pipeline_mode=resident),
            _layer_block(w2, layer, pipeline_mode=resident),
        ],
        out_specs=pl.BlockSpec((tm, d), lambda ri: (ri, 0)),
        out_shape=jax.ShapeDtypeStruct((rows, d), F32),
        compiler_params=_compiler_params(("arbitrary",), 2 * tile, weights, 6 * tile),
        name="mlp_sublayer",
    )(x.reshape(rows, d), gains, w1, w2)
    return out.reshape(b, s, d)


def kernel(x, mem, norm_gains, mem_norm, pool_w_in, pool_w_group, pool_scale, moba_w_qkv, moba_w_o,
           xa_w_q, xa_w_kv, xa_w_o, mlp_w1, mlp_w2):
    depth = norm_gains.shape[0]
    s, d = x.shape[1], x.shape[2]
    assert norm_gains.shape[1] == N_NORMS
    rope = _rope_tables(s)
    scale = pool_scale.reshape(pool_scale.shape[0], 1, d)
    kv_all = _mem_kv(mem, mem_norm, xa_w_kv)
    for i in range(depth):
        j = i // 2
        if i % 2 == 0:
            x = _pool_layer(x, norm_gains, i, 0, pool_w_in, pool_w_group, scale, j)
        else:
            q, k2, vt, kmean = _moba_qkv(x, norm_gains, i, 0, moba_w_qkv, j, rope)
            x = _moba_attn(x, q, k2, vt, kmean, moba_w_o, j, norm_gains, i, 1)
        x = _xattn_layer(x, norm_gains, i, 2, xa_w_q, kv_all, xa_w_o)
        x = _mlp_layer(x, norm_gains, i, 4, mlp_w1, mlp_w2)
    return x
```

```python
import functools

import jax
import jax.numpy as jnp
from jax import lax
from jax.experimental import pallas as pl
from jax.experimental.pallas import tpu as pltpu

F32 = jnp.float32
BF16 = jnp.bfloat16

RMS_EPS = 1e-6
N_NORMS = 6
POOL_WINDOWS = (2, 4, 8, 16)
POOL_GW = 256
POOL_HALO = 16
MOBA_HEADS = 8
HEAD_DIM = 128
ROT_DIM = 32
ROPE_THETA = 500000.0
MOBA_BLOCK = 256
MOBA_TOPK = 3
MEM_HEADS = 4
MEM_HEAD_DIM = 256

TOKEN_TILE = 512
XATTN_TILE = 1024
ATTN_QBLOCKS = 2
SUB_ROWS = 256
FF_CHUNK = 1024
V7X_VMEM_BYTES = 64 * 1024 * 1024
V7X_VMEM_USABLE = 56 * 1024 * 1024

GATE_ROWS = 16
MASK_BIAS = -1e30
SUM_ROWS = 16
LOG2_E = 1.4426950408889634

NT_DIMS = (((1,), (1,)), ((), ()))
TN_DIMS = (((0,), (0,)), ((), ()))


def _nbytes(shape, dtype):
    n = 1
    for d in shape:
        n *= d
    return n * jnp.dtype(dtype).itemsize


def _compiler_params(semantics, pipelined_bytes, resident_bytes, temp_bytes):
    need = 2 * pipelined_bytes + resident_bytes + temp_bytes
    return pltpu.CompilerParams(
        dimension_semantics=semantics,
        vmem_limit_bytes=int(min(max(need, 16 * 1024 * 1024), V7X_VMEM_USABLE)),
    )


def _layer_block(stack, layer, **kwargs):
    shape = (1,) + tuple(stack.shape[1:])
    zeros = (0,) * (stack.ndim - 1)
    return pl.BlockSpec(shape, lambda *_: (layer,) + zeros, **kwargs)


def _sub_tiles(rows):
    assert rows % SUB_ROWS == 0
    return [slice(r, r + SUB_ROWS) for r in range(0, rows, SUB_ROWS)]


def _rms(xf, gain):
    ms = jnp.mean(xf * xf, axis=-1, keepdims=True)
    return xf * lax.rsqrt(ms + RMS_EPS) * gain


def _pool_kernel(x_ref, g_ref, win_ref, wg_ref, scale_ref, o_ref, carry_ref, *, tm, gk):
    s = pl.program_id(1)

    @pl.when(s == 0)
    def _():
        carry_ref[...] = jnp.zeros_like(carry_ref)

    x = x_ref[0]
    hn = _rms(x, g_ref[0, gk:gk + 1, :]).astype(BF16)
    u = jnp.dot(hn, win_ref[0].astype(BF16), preferred_element_type=F32)
    ext = jnp.concatenate([carry_ref[...], u], axis=0)
    carry_ref[...] = u[tm - POOL_HALO:, :]

    pos = s * tm + lax.broadcasted_iota(jnp.int32, (tm, 1), 0)
    ys = []
    for g, w in enumerate(POOL_WINDOWS):
        cols = slice(g * POOL_GW, (g + 1) * POOL_GW)
        t = ext[:, cols]
        shift = 1
        while shift < w:
            t = t + pltpu.roll(t, shift, 0)
            shift *= 2
        win_sum = t[POOL_HALO:, :]
        cnt = jnp.minimum(pos + 1, w).astype(F32)
        pooled = win_sum / cnt - u[:, cols]
        ys.append(jnp.dot(pooled.astype(BF16), wg_ref[0, g].astype(BF16), preferred_element_type=F32))
    y = jnp.concatenate(ys, axis=1) * scale_ref[0]
    o_ref[0] = x + _rms(y, g_ref[0, gk + 1:gk + 2, :])


def _pool_layer(x, gains, layer, gk, w_in, w_group, scale, j):
    b, s, d = x.shape
    tm = TOKEN_TILE
    assert s % tm == 0 and tm % 8 == 0 and tm >= POOL_HALO
    tile = _nbytes((tm, d), F32)
    weights = _nbytes(w_in.shape[1:], F32) + _nbytes(w_group.shape[1:], F32)
    return pl.pallas_call(
        functools.partial(_pool_kernel, tm=tm, gk=gk),
        grid=(b, s // tm),
        in_specs=[
            pl.BlockSpec((1, tm, d), lambda bi, si: (bi, si, 0)),
            _layer_block(gains, layer),
            _layer_block(w_in, j),
            _layer_block(w_group, j),
            _layer_block(scale, j),
        ],
        out_specs=pl.BlockSpec((1, tm, d), lambda bi, si: (bi, si, 0)),
        out_shape=jax.ShapeDtypeStruct(x.shape, F32),
        scratch_shapes=[pltpu.VMEM((POOL_HALO, d), F32)],
        compiler_params=_compiler_params(("arbitrary", "arbitrary"), 2 * tile + weights, 0, 8 * tile),
        name="pool_sublayer",
    )(x, gains, w_in, w_group, scale)


def _rope_tables(s):
    half = ROT_DIM // 2
    pos = jnp.arange(s, dtype=F32)
    inv_freq = ROPE_THETA ** (-jnp.arange(0, ROT_DIM, 2, dtype=F32) / ROT_DIM)
    ang = pos[:, None] * inv_freq[None, :]
    cos, sin = jnp.cos(ang), jnp.sin(ang)
    pad = HEAD_DIM - ROT_DIM
    c = jnp.concatenate([cos, cos, jnp.ones((s, pad), F32)], axis=1)
    s_lo = jnp.concatenate([-sin, jnp.zeros((s, HEAD_DIM - half), F32)], axis=1)
    s_hi = jnp.concatenate([jnp.zeros((s, half), F32), sin, jnp.zeros((s, pad), F32)], axis=1)
    k_tab = jnp.stack([c, s_lo, s_hi])
    return jnp.concatenate([k_tab * (LOG2_E * HEAD_DIM ** -0.5), k_tab], axis=0)


def _qkv_kernel(x_ref, g_ref, wq_ref, wk_ref, wv_ref, rope_ref, q_ref, k2_ref, vt_ref, kmean_ref, *, tm, gk):
    s = pl.program_id(1)
    half = ROT_DIM // 2
    blocks_per_tile = tm // MOBA_BLOCK
    hn = _rms(x_ref[0], g_ref[0, gk:gk + 1, :]).astype(BF16)
    for c in range(2 * MOBA_HEADS):
        if c % 2 == 0:
            w_ref = wq_ref if c < MOBA_HEADS else wk_ref
            c0 = c % MOBA_HEADS
            w2 = w_ref[0, :, c0 * HEAD_DIM:(c0 + 2) * HEAD_DIM].astype(BF16)
            y2 = jnp.dot(hn, w2, preferred_element_type=F32)
        y = y2[:, (c % 2) * HEAD_DIM:(c % 2 + 1) * HEAD_DIM]
        t = 0 if c < MOBA_HEADS else 3
        y = (y * rope_ref[t]
             + pltpu.roll(y, HEAD_DIM - half, 1) * rope_ref[t + 1]
             + pltpu.roll(y, half, 1) * rope_ref[t + 2])
        if c < MOBA_HEADS:
            q_ref[0, c] = y.astype(BF16)
        else:
            h = c - MOBA_HEADS
            for r in range(blocks_per_tile):
                row = jnp.mean(y[r * MOBA_BLOCK:(r + 1) * MOBA_BLOCK, :], axis=0, keepdims=True)
                kmean_ref[0, h, pl.ds(s * blocks_per_tile + r, 1), :] = row
            if h % 2 == 0:
                k_even = y
            else:
                k2_ref[0, h // 2] = jnp.concatenate([k_even, y], axis=1).astype(BF16)
    vt = lax.dot_general(wv_ref[0].astype(BF16), hn, (((0,), (1,)), ((), ())), preferred_element_type=F32)
    vt_ref[0] = vt.reshape(MOBA_HEADS, HEAD_DIM, tm).astype(BF16)


def _moba_qkv(x, gains, layer, gk, w_qkv, j, rope):
    b, s, d = x.shape
    tm = TOKEN_TILE
    nb = s // MOBA_BLOCK
    h = MOBA_HEADS
    assert s % tm == 0 and tm % MOBA_BLOCK == 0 and d == h * HEAD_DIM and w_qkv.shape[1:] == (d, 3 * d)
    tile = _nbytes((tm, d), F32)
    out_tile = _nbytes((3 * h, tm, HEAD_DIM), BF16)
    rope_tile = _nbytes((6, tm, HEAD_DIM), F32)
    weights = _nbytes((d, 3 * d), F32)
    w_part = lambda part: pl.BlockSpec((1, d, d), lambda bi, si: (j, 0, part))
    return pl.pallas_call(
        functools.partial(_qkv_kernel, tm=tm, gk=gk),
        grid=(b, s // tm),
        in_specs=[
            pl.BlockSpec((1, tm, d), lambda bi, si: (bi, si, 0)),
            _layer_block(gains, layer),
            w_part(0), w_part(1), w_part(2),
            pl.BlockSpec((6, tm, HEAD_DIM), lambda bi, si: (0, si, 0)),
        ],
        out_specs=[
            pl.BlockSpec((1, h, tm, HEAD_DIM), lambda bi, si: (bi, 0, si, 0)),
            pl.BlockSpec((1, h // 2, tm, 2 * HEAD_DIM), lambda bi, si: (bi, 0, si, 0)),
            pl.BlockSpec((1, h, HEAD_DIM, tm), lambda bi, si: (bi, 0, 0, si)),
            pl.BlockSpec((1, h, nb, HEAD_DIM), lambda bi, si: (bi, 0, 0, 0)),
        ],
        out_shape=[
            jax.ShapeDtypeStruct((b, h, s, HEAD_DIM), BF16),
            jax.ShapeDtypeStruct((b, h // 2, s, 2 * HEAD_DIM), BF16),
            jax.ShapeDtypeStruct((b, h, HEAD_DIM, s), BF16),
            jax.ShapeDtypeStruct((b, h, nb, HEAD_DIM), F32),
        ],
        compiler_params=_compiler_params(
            ("arbitrary", "arbitrary"), tile + out_tile + rope_tile + weights, 0, 4 * tile),
        name="moba_qkv",
    )(x, gains, w_qkv, w_qkv, w_qkv, rope)


def _attn_query_block(q_ref, k2_ref, vt_ref, km_ref, ot_ref, ii, qb):
    blk = MOBA_BLOCK
    hd = HEAD_DIM
    nb = k2_ref.shape[2] // blk
    nbp = GATE_ROWS
    gated = ii > MOBA_TOPK
    q_rows = slice(qb * blk, (qb + 1) * blk)

    key_id = lax.broadcasted_iota(jnp.int32, (blk, 2 * blk), 0)
    qry_id = lax.broadcasted_iota(jnp.int32, (blk, 2 * blk), 1) & (blk - 1)
    causal = key_id <= qry_id
    blk_id = lax.broadcasted_iota(jnp.int32, (nbp, blk), 0)
    zeros_q = jnp.zeros((blk, hd), BF16)
    ones = jnp.ones((SUM_ROWS, blk), BF16)

    def block_bias(q, h):
        km = km_ref[0, h]
        km = jnp.concatenate([km, jnp.zeros((nbp - nb, hd), F32)], axis=0)
        km_hi = km.astype(BF16)
        km_lo = (km - km_hi.astype(F32)).astype(BF16)
        gate = (lax.dot_general(km_hi, q, NT_DIMS, preferred_element_type=F32)
                + lax.dot_general(km_lo, q, NT_DIMS, preferred_element_type=F32))
        rank = jnp.zeros((nbp, blk), F32)
        for jp in range(ii):
            gj = gate[jp:jp + 1, :]
            beats = (gj > gate) | ((gj == gate) & (blk_id > jp))
            rank = rank + beats.astype(F32)
        return jnp.where(rank >= float(MOBA_TOPK), MASK_BIAS, 0.0)

    def pair(g, carry):
        q0 = q_ref[0, 2 * g, q_rows, :]
        q1 = q_ref[0, 2 * g + 1, q_rows, :]
        qd = jnp.concatenate([jnp.concatenate([q0, zeros_q], axis=1),
                              jnp.concatenate([zeros_q, q1], axis=1)], axis=0)
        if gated:
            bias = jnp.concatenate([block_bias(q0, 2 * g), block_bias(q1, 2 * g + 1)], axis=1)
        m = None
        acc = [None, None]
        order = [ii] + list(range(ii))

        def scores(j):
            sc = lax.dot_general(k2_ref[0, g, j * blk:(j + 1) * blk, :], qd, NT_DIMS,
                                 preferred_element_type=F32)
            if j == ii:
                sc = jnp.where(causal, sc, -jnp.inf)
            bm = jnp.max(sc, axis=0, keepdims=True)
            if gated and j < ii:
                bm = bm + bias[j:j + 1, :]
            return sc, bm

        def accumulate(j, p, alpha):
            for hh in range(2):
                cols = slice(hh * blk, (hh + 1) * blk)
                vt1 = jnp.concatenate([vt_ref[0, 2 * g + hh, :, j * blk:(j + 1) * blk], ones], axis=0)
                pv = jnp.dot(vt1, p[:, cols], preferred_element_type=F32)
                acc[hh] = pv if alpha is None else acc[hh] * alpha[:, cols] + pv

        n_blk = len(order)
        sc_q = [scores(order[0])]
        if n_blk > 1:
            sc_q.append(scores(order[1]))
        pending = None
        for t, j in enumerate(order):
            sc, bm = sc_q.pop(0)
            if t + 2 < n_blk:
                sc_q.append(scores(order[t + 2]))
            if pending is not None:
                accumulate(*pending)
            masked = gated and j < ii
            m_new = bm if m is None else jnp.maximum(m, bm)
            shift = bias[j:j + 1, :] - m_new if masked else -m_new
            p = jnp.exp2(sc + shift).astype(BF16)
            alpha = None if m is None else jnp.exp2(m - m_new)
            pending = (j, p, alpha)
            m = m_new
        accumulate(*pending)
        for hh in range(2):
            ot_ref[qb, 2 * g + hh] = (acc[hh][:hd, :] / acc[hh][hd:hd + 1, :]).astype(BF16)
        return carry

    lax.fori_loop(0, MOBA_HEADS // 2, pair, 0, unroll=True)


def _attn_kernel(q_ref, k2_ref, vt_ref, km_ref, x_ref, wo_ref, g_ref, o_ref, ot_ref, *, gk):
    i = pl.program_id(1)
    blk = MOBA_BLOCK
    nb = k2_ref.shape[2] // blk
    d = x_ref.shape[2]

    for step in range(nb // ATTN_QBLOCKS):
        for qb in range(ATTN_QBLOCKS):
            pl.when(i == step)(functools.partial(
                _attn_query_block, q_ref, k2_ref, vt_ref, km_ref, ot_ref, step * ATTN_QBLOCKS + qb, qb))

    o_t = jnp.concatenate([ot_ref[qb].reshape(d, blk) for qb in range(ATTN_QBLOCKS)], axis=1)
    y = lax.dot_general(o_t, wo_ref[0].astype(BF16), TN_DIMS, preferred_element_type=F32)
    o_ref[0] = x_ref[0] + _rms(y, g_ref[0, gk:gk + 1, :])


def _moba_attn(x, q, k2, vt, kmean, w_o, j, gains, layer, gk):
    b, s, d = x.shape
    blk = MOBA_BLOCK
    nb = s // blk
    h = MOBA_HEADS
    rows = ATTN_QBLOCKS * blk
    assert nb <= GATE_ROWS and blk == 256 and h % 2 == 0 and nb % ATTN_QBLOCKS == 0
    q_tile = _nbytes((h, rows, HEAD_DIM), BF16)
    kv_tile = _nbytes((h, s, HEAD_DIM), BF16)
    x_tile = _nbytes((rows, d), F32)
    return pl.pallas_call(
        functools.partial(_attn_kernel, gk=gk),
        grid=(b, nb // ATTN_QBLOCKS),
        in_specs=[
            pl.BlockSpec((1, h, rows, HEAD_DIM), lambda bi, qi: (bi, 0, qi, 0)),
            pl.BlockSpec((1, h // 2, s, 2 * HEAD_DIM), lambda bi, qi: (bi, 0, 0, 0)),
            pl.BlockSpec((1, h, HEAD_DIM, s), lambda bi, qi: (bi, 0, 0, 0)),
            pl.BlockSpec((1, h, nb, HEAD_DIM), lambda bi, qi: (bi, 0, 0, 0)),
            pl.BlockSpec((1, rows, d), lambda bi, qi: (bi, qi, 0)),
            _layer_block(w_o, j),
            _layer_block(gains, layer),
        ],
        out_specs=pl.BlockSpec((1, rows, d), lambda bi, qi: (bi, qi, 0)),
        out_shape=jax.ShapeDtypeStruct(x.shape, F32),
        scratch_shapes=[pltpu.VMEM((ATTN_QBLOCKS, h, HEAD_DIM, blk), BF16)],
        compiler_params=_compiler_params(
            ("arbitrary", "arbitrary"), q_tile + 2 * kv_tile + 2 * x_tile + _nbytes((d, d), F32),
            x_tile, 8 * _nbytes((blk, s), F32)),
        name="moba_attention",
    )(q, k2, vt, kmean, x, w_o, gains)


def _memkv_kernel(mem_ref, g_ref, w_ref, o_ref):
    mn = _rms(mem_ref[0], g_ref[0]).astype(BF16)
    o_ref[0, 0] = jnp.dot(mn, w_ref[0].astype(BF16), preferred_element_type=F32).astype(BF16)


def _mem_kv(mem, mem_norm, w_kv):
    depth, d, d2 = w_kv.shape
    b, m, _ = mem.shape
    return pl.pallas_call(
        _memkv_kernel,
        grid=(depth, b),
        in_specs=[
            pl.BlockSpec((1, m, d), lambda li, bi: (bi, 0, 0)),
            pl.BlockSpec((1, 1, d), lambda li, bi: (li, 0, 0)),
            pl.BlockSpec((1, d, d2), lambda li, bi: (li, 0, 0)),
        ],
        out_specs=pl.BlockSpec((1, 1, m, d2), lambda li, bi: (li, bi, 0, 0)),
        out_shape=jax.ShapeDtypeStruct((depth, b, m, d2), BF16),
        compiler_params=_compiler_params(
            ("arbitrary", "arbitrary"),
            _nbytes((m, d), F32) + _nbytes((d, d2), F32) + _nbytes((m, d2), BF16), 0, 4 * _nbytes((m, d2), F32)),
        name="memory_kv",
    )(mem, mem_norm.reshape(depth, 1, d), w_kv)


def _xattn_kernel(x_ref, g_ref, wq_ref, kv_ref, wo_ref, o_ref, *, gk):
    d = x_ref.shape[2]
    hd = MEM_HEAD_DIM
    subs = _sub_tiles(x_ref.shape[1])
    heads = range(MEM_HEADS)
    xs = [x_ref[0, rows, :] for rows in subs]
    hn = [_rms(x, g_ref[0, gk:gk + 1, :]).astype(BF16) for x in xs]
    wq = wq_ref[0].astype(BF16)
    q_scale = LOG2_E * hd ** -0.5
    q = [(jnp.dot(h, wq, preferred_element_type=F32) * q_scale).astype(BF16) for h in hn]
    sc = [[lax.dot_general(qs[:, h * hd:(h + 1) * hd], kv_ref[0, 0, :, h * hd:(h + 1) * hd], NT_DIMS,
                           preferred_element_type=F32) for h in heads] for qs in q]
    p, inv_l = [], []
    for sc_s in sc:
        e = [jnp.exp2(t - jnp.max(t, axis=1, keepdims=True)) for t in sc_s]
        inv_l.append([1.0 / jnp.sum(t, axis=1, keepdims=True) for t in e])
        p.append([t.astype(BF16) for t in e])
    wo = wo_ref[0].astype(BF16)
    for rows, x, p_s, inv_s in zip(subs, xs, p, inv_l):
        outs = [jnp.dot(p_s[h], kv_ref[0, 0, :, d + h * hd:d + (h + 1) * hd], preferred_element_type=F32) * inv_s[h]
                for h in heads]
        o = jnp.concatenate(outs, axis=1).astype(BF16)
        y = jnp.dot(o, wo, preferred_element_type=F32)
        o_ref[0, rows, :] = x + _rms(y, g_ref[0, gk + 1:gk + 2, :])


def _xattn_layer(x, gains, layer, gk, w_q, kv_all, w_o):
    b, s, d = x.shape
    tm = XATTN_TILE
    m = kv_all.shape[2]
    assert s % tm == 0 and d == MEM_HEADS * MEM_HEAD_DIM
    tile = _nbytes((tm, d), F32)
    return pl.pallas_call(
        functools.partial(_xattn_kernel, gk=gk),
        grid=(b, s // tm),
        in_specs=[
            pl.BlockSpec((1, tm, d), lambda bi, si: (bi, si, 0)),
            _layer_block(gains, layer),
            _layer_block(w_q, layer),
            pl.BlockSpec((1, 1, m, 2 * d), lambda bi, si: (layer, bi, 0, 0)),
            _layer_block(w_o, layer),
        ],
        out_specs=pl.BlockSpec((1, tm, d), lambda bi, si: (bi, si, 0)),
        out_shape=jax.ShapeDtypeStruct(x.shape, F32),
        compiler_params=_compiler_params(
            ("arbitrary", "arbitrary"),
            2 * tile + 2 * _nbytes((d, d), F32) + _nbytes((m, 2 * d), BF16), 0, 6 * tile),
        name="memory_xattn_sublayer",
    )(x, gains, w_q, kv_all, w_o)


def _mlp_kernel(x_ref, g_ref, w1_hbm, w2_hbm, o_ref, w1_ref, w2_ref, sem, *, gk, layer):
    d_ff = w1_ref.shape[1]
    n_chunks = d_ff // FF_CHUNK
    step = pl.program_id(0)

    def chunk_copies(c):
        ff = pl.ds(c * FF_CHUNK, FF_CHUNK)
        return (pltpu.make_async_copy(w1_hbm.at[layer, :, ff], w1_ref.at[:, ff], sem.at[0, c]),
                pltpu.make_async_copy(w2_hbm.at[layer, ff, :], w2_ref.at[ff, :], sem.at[1, c]))

    @pl.when(step == 0)
    def _():
        for c in range(n_chunks):
            for copy in chunk_copies(c):
                copy.start()

    def body(wait_for_weights):
        subs = _sub_tiles(x_ref.shape[0])
        xs = [x_ref[rows, :] for rows in subs]
        hn = [_rms(x, g_ref[0, gk:gk + 1, :]).astype(BF16) for x in xs]
        acc = [jnp.zeros(x.shape, F32) for x in xs]
        for c in range(n_chunks):
            if wait_for_weights:
                for copy in chunk_copies(c):
                    copy.wait()
            cols = slice(c * FF_CHUNK, (c + 1) * FF_CHUNK)
            w1c = w1_ref[:, cols].astype(BF16)
            w2c = w2_ref[cols, :].astype(BF16)
            a = [jnp.maximum(jnp.dot(h, w1c, preferred_element_type=F32), 0.0) for h in hn]
            acc = [t + jnp.dot((u * u).astype(BF16), w2c, preferred_element_type=F32) for t, u in zip(acc, a)]
        for rows, x, t in zip(subs, xs, acc):
            o_ref[rows, :] = x + _rms(t, g_ref[0, gk + 1:gk + 2, :])

    pl.when(step == 0)(functools.partial(body, True))
    pl.when(step != 0)(functools.partial(body, False))


def _mlp_layer(x, gains, layer, gk, w1, w2):
    b, s, d = x.shape
    tm = TOKEN_TILE
    rows = b * s
    d_ff = w1.shape[2]
    assert rows % tm == 0 and d_ff % FF_CHUNK == 0
    tile = _nbytes((tm, d), F32)
    weights = 2 * _nbytes((d, d_ff), F32)
    out = pl.pallas_call(
        functools.partial(_mlp_kernel, gk=gk, layer=layer),
        grid=(rows // tm,),
        in_specs=[
            pl.BlockSpec((tm, d), lambda ri: (ri, 0)),
            _layer_block(gains, layer),
            pl.BlockSpec(memory_space=pl.ANY),
            pl.BlockSpec(memory_space=pl.ANY),
        ],
        out_specs=pl.BlockSpec((tm, d), lambda ri: (ri, 0)),
        out_shape=jax.ShapeDtypeStruct((rows, d), F32),
        scratch_shapes=[
            pltpu.VMEM((d, d_ff), F32),
            pltpu.VMEM((d_ff, d), F32),
            pltpu.SemaphoreType.DMA((2, d_ff // FF_CHUNK)),
        ],
        compiler_params=_compiler_params(("arbitrary",), 2 * tile, weights, 6 * tile),
        name="mlp_sublayer",
    )(x.reshape(rows, d), gains, w1, w2)
    return out.reshape(b, s, d)


def kernel(x, mem, norm_gains, mem_norm, pool_w_in, pool_w_group, pool_scale, moba_w_qkv, moba_w_o,
           xa_w_q, xa_w_kv, xa_w_o, mlp_w1, mlp_w2):
    depth = norm_gains.shape[0]
    s, d = x.shape[1], x.shape[2]
    assert norm_gains.shape[1] == N_NORMS
    rope = _rope_tables(s)
    scale = pool_scale.reshape(pool_scale.shape[0], 1, d)
    kv_all = _mem_kv(mem, mem_norm, xa_w_kv)
    for i in range(depth):
        j = i // 2
        if i % 2 == 0:
            x = _pool_layer(x, norm_gains, i, 0, pool_w_in, pool_w_group, scale, j)
        else:
            q, k2, vt, kmean = _moba_qkv(x, norm_gains, i, 0, moba_w_qkv, j, rope)
            x = _moba_attn(x, q, k2, vt, kmean, moba_w_o, j, norm_gains, i, 1)
        x = _xattn_layer(x, norm_gains, i, 2, xa_w_q, kv_all, xa_w_o)
        x = _mlp_layer(x, norm_gains, i, 4, mlp_w1, mlp_w2)
    return x
```

```python
import functools

import jax
import jax.numpy as jnp
from jax import lax
from jax.experimental import pallas as pl
from jax.experimental.pallas import tpu as pltpu

F32 = jnp.float32
BF16 = jnp.bfloat16

RMS_EPS = 1e-6
N_NORMS = 6
POOL_WINDOWS = (2, 4, 8, 16)
POOL_GW = 256
POOL_HALO = 16
MOBA_HEADS = 8
HEAD_DIM = 128
ROT_DIM = 32
ROPE_THETA = 500000.0
MOBA_BLOCK = 256
MOBA_TOPK = 3
MEM_HEADS = 4
MEM_HEAD_DIM = 256

TOKEN_TILE = 512
XATTN_TILE = 1024
ATTN_QBLOCKS = 2
MEMKV_BATCH = 2
SUB_ROWS = 256
FF_CHUNK = 1024
V7X_VMEM_BYTES = 64 * 1024 * 1024
V7X_VMEM_USABLE = 56 * 1024 * 1024

GATE_ROWS = 16
MASK_BIAS = -1e30
SUM_ROWS = 16
LOG2_E = 1.4426950408889634

NT_DIMS = (((1,), (1,)), ((), ()))
TN_DIMS = (((0,), (0,)), ((), ()))


def _nbytes(shape, dtype):
    n = 1
    for d in shape:
        n *= d
    return n * jnp.dtype(dtype).itemsize


def _compiler_params(semantics, pipelined_bytes, resident_bytes, temp_bytes):
    need = 2 * pipelined_bytes + resident_bytes + temp_bytes
    return pltpu.CompilerParams(
        dimension_semantics=semantics,
        vmem_limit_bytes=int(min(max(need, 16 * 1024 * 1024), V7X_VMEM_USABLE)),
    )


def _layer_block(stack, layer, **kwargs):
    shape = (1,) + tuple(stack.shape[1:])
    zeros = (0,) * (stack.ndim - 1)
    return pl.BlockSpec(shape, lambda *_: (layer,) + zeros, **kwargs)


def _sub_tiles(rows):
    assert rows % SUB_ROWS == 0
    return [slice(r, r + SUB_ROWS) for r in range(0, rows, SUB_ROWS)]


def _rms(xf, gain):
    ms = jnp.mean(xf * xf, axis=-1, keepdims=True)
    return xf * lax.rsqrt(ms + RMS_EPS) * gain


def _pool_kernel(x_ref, g_ref, win_ref, wg_ref, scale_ref, o_ref, carry_ref, *, tm, gk):
    s = pl.program_id(1)

    @pl.when(s == 0)
    def _():
        carry_ref[...] = jnp.zeros_like(carry_ref)

    subs = _sub_tiles(tm)
    xs = [x_ref[0, rows, :] for rows in subs]
    hn = [_rms(x, g_ref[0, gk:gk + 1, :]).astype(BF16) for x in xs]
    w_in = win_ref[0].astype(BF16)
    us = [jnp.dot(h, w_in, preferred_element_type=F32) for h in hn]
    halo = carry_ref[...]
    for rows, x, u in zip(subs, xs, us):
        ext = jnp.concatenate([halo, u], axis=0)
        halo = u[SUB_ROWS - POOL_HALO:, :]
        pos = s * tm + rows.start + lax.broadcasted_iota(jnp.int32, (SUB_ROWS, 1), 0)
        ys = []
        for g, w in enumerate(POOL_WINDOWS):
            cols = slice(g * POOL_GW, (g + 1) * POOL_GW)
            t = ext[:, cols]
            shift = 1
            while shift < w:
                t = t + pltpu.roll(t, shift, 0)
                shift *= 2
            win_sum = t[POOL_HALO:, :]
            cnt = jnp.minimum(pos + 1, w).astype(F32)
            pooled = win_sum / cnt - u[:, cols]
            ys.append(jnp.dot(pooled.astype(BF16), wg_ref[0, g].astype(BF16), preferred_element_type=F32))
        y = jnp.concatenate(ys, axis=1) * scale_ref[0]
        o_ref[0, rows, :] = x + _rms(y, g_ref[0, gk + 1:gk + 2, :])
    carry_ref[...] = halo


def _pool_layer(x, gains, layer, gk, w_in, w_group, scale, j):
    b, s, d = x.shape
    tm = TOKEN_TILE
    assert s % tm == 0 and tm % 8 == 0 and tm >= POOL_HALO
    tile = _nbytes((tm, d), F32)
    weights = _nbytes(w_in.shape[1:], F32) + _nbytes(w_group.shape[1:], F32)
    return pl.pallas_call(
        functools.partial(_pool_kernel, tm=tm, gk=gk),
        grid=(b, s // tm),
        in_specs=[
            pl.BlockSpec((1, tm, d), lambda bi, si: (bi, si, 0)),
            _layer_block(gains, layer),
            _layer_block(w_in, j),
            _layer_block(w_group, j),
            _layer_block(scale, j),
        ],
        out_specs=pl.BlockSpec((1, tm, d), lambda bi, si: (bi, si, 0)),
        out_shape=jax.ShapeDtypeStruct(x.shape, F32),
        scratch_shapes=[pltpu.VMEM((POOL_HALO, d), F32)],
        compiler_params=_compiler_params(("arbitrary", "arbitrary"), 2 * tile + weights, 0, 8 * tile),
        name="pool_sublayer",
    )(x, gains, w_in, w_group, scale)


def _rope_tables(s):
    half = ROT_DIM // 2
    pos = jnp.arange(s, dtype=F32)
    inv_freq = ROPE_THETA ** (-jnp.arange(0, ROT_DIM, 2, dtype=F32) / ROT_DIM)
    ang = pos[:, None] * inv_freq[None, :]
    cos, sin = jnp.cos(ang), jnp.sin(ang)
    pad = HEAD_DIM - ROT_DIM
    c = jnp.concatenate([cos, cos, jnp.ones((s, pad), F32)], axis=1)
    s_lo = jnp.concatenate([-sin, jnp.zeros((s, HEAD_DIM - half), F32)], axis=1)
    s_hi = jnp.concatenate([jnp.zeros((s, half), F32), sin, jnp.zeros((s, pad), F32)], axis=1)
    k_tab = jnp.stack([c, s_lo, s_hi])
    return jnp.concatenate([k_tab * (LOG2_E * HEAD_DIM ** -0.5), k_tab], axis=0)


def _qkv_kernel(x_ref, g_ref, wq_ref, wk_ref, wv_ref, rope_ref, q_ref, k2_ref, vt_ref, kmean_ref, *, tm, gk):
    s = pl.program_id(1)
    half = ROT_DIM // 2
    blocks_per_tile = tm // MOBA_BLOCK
    hn = _rms(x_ref[0], g_ref[0, gk:gk + 1, :]).astype(BF16)
    for c in range(2 * MOBA_HEADS):
        if c % 2 == 0:
            w_ref = wq_ref if c < MOBA_HEADS else wk_ref
            c0 = c % MOBA_HEADS
            w2 = w_ref[0, :, c0 * HEAD_DIM:(c0 + 2) * HEAD_DIM].astype(BF16)
            y2 = jnp.dot(hn, w2, preferred_element_type=F32)
        y = y2[:, (c % 2) * HEAD_DIM:(c % 2 + 1) * HEAD_DIM]
        t = 0 if c < MOBA_HEADS else 3
        y = (y * rope_ref[t]
             + pltpu.roll(y, HEAD_DIM - half, 1) * rope_ref[t + 1]
             + pltpu.roll(y, half, 1) * rope_ref[t + 2])
        if c < MOBA_HEADS:
            q_ref[0, c] = y.astype(BF16)
        else:
            h = c - MOBA_HEADS
            for r in range(blocks_per_tile):
                row = jnp.mean(y[r * MOBA_BLOCK:(r + 1) * MOBA_BLOCK, :], axis=0, keepdims=True)
                kmean_ref[0, h, pl.ds(s * blocks_per_tile + r, 1), :] = row
            if h % 2 == 0:
                k_even = y
            else:
                k2_ref[0, h // 2] = jnp.concatenate([k_even, y], axis=1).astype(BF16)
    vt = lax.dot_general(wv_ref[0].astype(BF16), hn, (((0,), (1,)), ((), ())), preferred_element_type=F32)
    vt_ref[0] = vt.reshape(MOBA_HEADS, HEAD_DIM, tm).astype(BF16)


def _moba_qkv(x, gains, layer, gk, w_qkv, j, rope):
    b, s, d = x.shape
    tm = TOKEN_TILE
    nb = s // MOBA_BLOCK
    h = MOBA_HEADS
    assert s % tm == 0 and tm % MOBA_BLOCK == 0 and d == h * HEAD_DIM and w_qkv.shape[1:] == (d, 3 * d)
    tile = _nbytes((tm, d), F32)
    out_tile = _nbytes((3 * h, tm, HEAD_DIM), BF16)
    rope_tile = _nbytes((6, tm, HEAD_DIM), F32)
    weights = _nbytes((d, 3 * d), F32)
    w_part = lambda part: pl.BlockSpec((1, d, d), lambda bi, si: (j, 0, part))
    return pl.pallas_call(
        functools.partial(_qkv_kernel, tm=tm, gk=gk),
        grid=(b, s // tm),
        in_specs=[
            pl.BlockSpec((1, tm, d), lambda bi, si: (bi, si, 0)),
            _layer_block(gains, layer),
            w_part(0), w_part(1), w_part(2),
            pl.BlockSpec((6, tm, HEAD_DIM), lambda bi, si: (0, si, 0)),
        ],
        out_specs=[
            pl.BlockSpec((1, h, tm, HEAD_DIM), lambda bi, si: (bi, 0, si, 0)),
            pl.BlockSpec((1, h // 2, tm, 2 * HEAD_DIM), lambda bi, si: (bi, 0, si, 0)),
            pl.BlockSpec((1, h, HEAD_DIM, tm), lambda bi, si: (bi, 0, 0, si)),
            pl.BlockSpec((1, h, nb, HEAD_DIM), lambda bi, si: (bi, 0, 0, 0)),
        ],
        out_shape=[
            jax.ShapeDtypeStruct((b, h, s, HEAD_DIM), BF16),
            jax.ShapeDtypeStruct((b, h // 2, s, 2 * HEAD_DIM), BF16),
            jax.ShapeDtypeStruct((b, h, HEAD_DIM, s), BF16),
            jax.ShapeDtypeStruct((b, h, nb, HEAD_DIM), F32),
        ],
        compiler_params=_compiler_params(
            ("arbitrary", "arbitrary"), tile + out_tile + rope_tile + weights, 0, 4 * tile),
        name="moba_qkv",
    )(x, gains, w_qkv, w_qkv, w_qkv, rope)


def _attn_query_block(q_ref, k2_ref, vt_ref, km_ref, ot_ref, ii, qb):
    blk = MOBA_BLOCK
    hd = HEAD_DIM
    nb = k2_ref.shape[2] // blk
    nbp = GATE_ROWS
    gated = ii > MOBA_TOPK
    q_rows = slice(qb * blk, (qb + 1) * blk)

    key_id = lax.broadcasted_iota(jnp.int32, (blk, 2 * blk), 0)
    qry_id = lax.broadcasted_iota(jnp.int32, (blk, 2 * blk), 1) & (blk - 1)
    causal = key_id <= qry_id
    blk_id = lax.broadcasted_iota(jnp.int32, (nbp, blk), 0)
    zeros_q = jnp.zeros((blk, hd), BF16)
    ones = jnp.ones((SUM_ROWS, blk), BF16)

    def block_bias(q, h):
        km = km_ref[0, h]
        km = jnp.concatenate([km, jnp.zeros((nbp - nb, hd), F32)], axis=0)
        km_hi = km.astype(BF16)
        km_lo = (km - km_hi.astype(F32)).astype(BF16)
        gate = (lax.dot_general(km_hi, q, NT_DIMS, preferred_element_type=F32)
                + lax.dot_general(km_lo, q, NT_DIMS, preferred_element_type=F32))
        rank = jnp.zeros((nbp, blk), F32)
        for jp in range(ii):
            gj = gate[jp:jp + 1, :]
            beats = (gj > gate) | ((gj == gate) & (blk_id > jp))
            rank = rank + beats.astype(F32)
        return jnp.where(rank >= float(MOBA_TOPK), MASK_BIAS, 0.0)

    def pair(g, carry):
        q0 = q_ref[0, 2 * g, q_rows, :]
        q1 = q_ref[0, 2 * g + 1, q_rows, :]
        qd = jnp.concatenate([jnp.concatenate([q0, zeros_q], axis=1),
                              jnp.concatenate([zeros_q, q1], axis=1)], axis=0)
        if gated:
            bias = jnp.concatenate([block_bias(q0, 2 * g), block_bias(q1, 2 * g + 1)], axis=1)
        m = None
        acc = [None, None]
        order = [ii] + list(range(ii))

        def scores(j):
            sc = lax.dot_general(k2_ref[0, g, j * blk:(j + 1) * blk, :], qd, NT_DIMS,
                                 preferred_element_type=F32)
            if j == ii:
                sc = jnp.where(causal, sc, -jnp.inf)
            bm = jnp.max(sc, axis=0, keepdims=True)
            if gated and j < ii:
                bm = bm + bias[j:j + 1, :]
            return sc, bm

        def accumulate(j, p, alpha):
            for hh in range(2):
                cols = slice(hh * blk, (hh + 1) * blk)
                vt1 = jnp.concatenate([vt_ref[0, 2 * g + hh, :, j * blk:(j + 1) * blk], ones], axis=0)
                pv = jnp.dot(vt1, p[:, cols], preferred_element_type=F32)
                acc[hh] = pv if alpha is None else acc[hh] * alpha[:, cols] + pv

        n_blk = len(order)
        sc_q = [scores(order[0])]
        if n_blk > 1:
            sc_q.append(scores(order[1]))
        pending = None
        for t, j in enumerate(order):
            sc, bm = sc_q.pop(0)
            if t + 2 < n_blk:
                sc_q.append(scores(order[t + 2]))
            if pending is not None:
                accumulate(*pending)
            masked = gated and j < ii
            m_new = bm if m is None else jnp.maximum(m, bm)
            shift = bias[j:j + 1, :] - m_new if masked else -m_new
            p = jnp.exp2(sc + shift).astype(BF16)
            alpha = None if m is None else jnp.exp2(m - m_new)
            pending = (j, p, alpha)
            m = m_new
        accumulate(*pending)
        for hh in range(2):
            ot_ref[qb, 2 * g + hh] = (acc[hh][:hd, :] / acc[hh][hd:hd + 1, :]).astype(BF16)
        return carry

    lax.fori_loop(0, MOBA_HEADS // 2, pair, 0, unroll=True)


def _attn_kernel(q_ref, k2_ref, vt_ref, km_ref, x_ref, wo_ref, g_ref, o_ref, ot_ref, *, gk):
    i = pl.program_id(1)
    blk = MOBA_BLOCK
    nb = k2_ref.shape[2] // blk
    d = x_ref.shape[2]

    def grid_step(step):
        for qb in range(ATTN_QBLOCKS):
            _attn_query_block(q_ref, k2_ref, vt_ref, km_ref, ot_ref, step * ATTN_QBLOCKS + qb, qb)
        o_t = jnp.concatenate([ot_ref[qb].reshape(d, blk) for qb in range(ATTN_QBLOCKS)], axis=1)
        y = lax.dot_general(o_t, wo_ref[0].astype(BF16), TN_DIMS, preferred_element_type=F32)
        o_ref[0] = x_ref[0] + _rms(y, g_ref[0, gk:gk + 1, :])

    for step in range(nb // ATTN_QBLOCKS):
        pl.when(i == step)(functools.partial(grid_step, step))


def _moba_attn(x, q, k2, vt, kmean, w_o, j, gains, layer, gk):
    b, s, d = x.shape
    blk = MOBA_BLOCK
    nb = s // blk
    h = MOBA_HEADS
    rows = ATTN_QBLOCKS * blk
    assert nb <= GATE_ROWS and blk == 256 and h % 2 == 0 and nb % ATTN_QBLOCKS == 0
    q_tile = _nbytes((h, rows, HEAD_DIM), BF16)
    kv_tile = _nbytes((h, s, HEAD_DIM), BF16)
    x_tile = _nbytes((rows, d), F32)
    return pl.pallas_call(
        functools.partial(_attn_kernel, gk=gk),
        grid=(b, nb // ATTN_QBLOCKS),
        in_specs=[
            pl.BlockSpec((1, h, rows, HEAD_DIM), lambda bi, qi: (bi, 0, qi, 0)),
            pl.BlockSpec((1, h // 2, s, 2 * HEAD_DIM), lambda bi, qi: (bi, 0, 0, 0)),
            pl.BlockSpec((1, h, HEAD_DIM, s), lambda bi, qi: (bi, 0, 0, 0)),
            pl.BlockSpec((1, h, nb, HEAD_DIM), lambda bi, qi: (bi, 0, 0, 0)),
            pl.BlockSpec((1, rows, d), lambda bi, qi: (bi, qi, 0)),
            _layer_block(w_o, j),
            _layer_block(gains, layer),
        ],
        out_specs=pl.BlockSpec((1, rows, d), lambda bi, qi: (bi, qi, 0)),
        out_shape=jax.ShapeDtypeStruct(x.shape, F32),
        scratch_shapes=[pltpu.VMEM((ATTN_QBLOCKS, h, HEAD_DIM, blk), BF16)],
        compiler_params=_compiler_params(
            ("arbitrary", "arbitrary"), q_tile + 2 * kv_tile + 2 * x_tile + _nbytes((d, d), F32),
            x_tile, 8 * _nbytes((blk, s), F32)),
        name="moba_attention",
    )(q, k2, vt, kmean, x, w_o, gains)


def _memkv_kernel(mem_ref, g_ref, w_ref, o_ref):
    nb, m, d = mem_ref.shape
    mn = _rms(mem_ref[...].reshape(nb * m, d), g_ref[0]).astype(BF16)
    kv = jnp.dot(mn, w_ref[0].astype(BF16), preferred_element_type=F32).astype(BF16)
    o_ref[0] = kv.reshape(nb, m, kv.shape[1])


def _mem_kv(mem, mem_norm, w_kv):
    depth, d, d2 = w_kv.shape
    b, m, _ = mem.shape
    nb = MEMKV_BATCH
    assert b % nb == 0 and m % 8 == 0
    return pl.pallas_call(
        _memkv_kernel,
        grid=(depth, b // nb),
        in_specs=[
            pl.BlockSpec((nb, m, d), lambda li, bi: (bi, 0, 0)),
            pl.BlockSpec((1, 1, d), lambda li, bi: (li, 0, 0)),
            pl.BlockSpec((1, d, d2), lambda li, bi: (li, 0, 0)),
        ],
        out_specs=pl.BlockSpec((1, nb, m, d2), lambda li, bi: (li, bi, 0, 0)),
        out_shape=jax.ShapeDtypeStruct((depth, b, m, d2), BF16),
        compiler_params=_compiler_params(
            ("arbitrary", "arbitrary"),
            _nbytes((nb * m, d), F32) + _nbytes((d, d2), F32) + _nbytes((nb * m, d2), BF16), 0,
            3 * _nbytes((nb * m, d2), F32)),
        name="memory_kv",
    )(mem, mem_norm.reshape(depth, 1, d), w_kv)


def _xattn_kernel(x_ref, g_ref, wq_ref, kv_ref, wo_ref, o_ref, *, gk):
    d = x_ref.shape[2]
    hd = MEM_HEAD_DIM
    subs = _sub_tiles(x_ref.shape[1])
    heads = range(MEM_HEADS)
    xs = [x_ref[0, rows, :] for rows in subs]
    hn = [_rms(x, g_ref[0, gk:gk + 1, :]).astype(BF16) for x in xs]
    wq = wq_ref[0].astype(BF16)
    q_scale = LOG2_E * hd ** -0.5
    q = [(jnp.dot(h, wq, preferred_element_type=F32) * q_scale).astype(BF16) for h in hn]
    sc = [[lax.dot_general(qs[:, h * hd:(h + 1) * hd], kv_ref[0, 0, :, h * hd:(h + 1) * hd], NT_DIMS,
                           preferred_element_type=F32) for h in heads] for qs in q]
    p, inv_l = [], []
    for sc_s in sc:
        e = [jnp.exp2(t - jnp.max(t, axis=1, keepdims=True)) for t in sc_s]
        inv_l.append([1.0 / jnp.sum(t, axis=1, keepdims=True) for t in e])
        p.append([t.astype(BF16) for t in e])
    wo = wo_ref[0].astype(BF16)
    for rows, x, p_s, inv_s in zip(subs, xs, p, inv_l):
        outs = [jnp.dot(p_s[h], kv_ref[0, 0, :, d + h * hd:d + (h + 1) * hd], preferred_element_type=F32) * inv_s[h]
                for h in heads]
        o = jnp.concatenate(outs, axis=1).astype(BF16)
        y = jnp.dot(o, wo, preferred_element_type=F32)
        o_ref[0, rows, :] = x + _rms(y, g_ref[0, gk + 1:gk + 2, :])


def _xattn_layer(x, gains, layer, gk, w_q, kv_all, w_o):
    b, s, d = x.shape
    tm = XATTN_TILE
    m = kv_all.shape[2]
    assert s % tm == 0 and d == MEM_HEADS * MEM_HEAD_DIM
    tile = _nbytes((tm, d), F32)
    return pl.pallas_call(
        functools.partial(_xattn_kernel, gk=gk),
        grid=(b, s // tm),
        in_specs=[
            pl.BlockSpec((1, tm, d), lambda bi, si: (bi, si, 0)),
            _layer_block(gains, layer),
            _layer_block(w_q, layer),
            pl.BlockSpec((1, 1, m, 2 * d), lambda bi, si: (layer, bi, 0, 0)),
            _layer_block(w_o, layer),
        ],
        out_specs=pl.BlockSpec((1, tm, d), lambda bi, si: (bi, si, 0)),
        out_shape=jax.ShapeDtypeStruct(x.shape, F32),
        compiler_params=_compiler_params(
            ("arbitrary", "arbitrary"),
            2 * tile + 2 * _nbytes((d, d), F32) + _nbytes((m, 2 * d), BF16), 0, 6 * tile),
        name="memory_xattn_sublayer",
    )(x, gains, w_q, kv_all, w_o)


def _mlp_kernel(x_ref, g_ref, w1_hbm, w2_hbm, o_ref, w1_ref, w2_ref, sem, *, gk, layer):
    d_ff = w1_ref.shape[1]
    n_chunks = d_ff // FF_CHUNK
    step = pl.program_id(0)

    def chunk_copies(c):
        ff = pl.ds(c * FF_CHUNK, FF_CHUNK)
        return (pltpu.make_async_copy(w1_hbm.at[layer, :, ff], w1_ref.at[:, ff], sem.at[0, c]),
                pltpu.make_async_copy(w2_hbm.at[layer, ff, :], w2_ref.at[ff, :], sem.at[1, c]))

    @pl.when(step == 0)
    def _():
        for c in range(n_chunks):
            for copy in chunk_copies(c):
                copy.start()

    def body(wait_for_weights):
        subs = _sub_tiles(x_ref.shape[0])
        xs = [x_ref[rows, :] for rows in subs]
        hn = [_rms(x, g_ref[0, gk:gk + 1, :]).astype(BF16) for x in xs]
        acc = [jnp.zeros(x.shape, F32) for x in xs]
        for c in range(n_chunks):
            if wait_for_weights:
                for copy in chunk_copies(c):
                    copy.wait()
            cols = slice(c * FF_CHUNK, (c + 1) * FF_CHUNK)
            w1c = w1_ref[:, cols].astype(BF16)
            w2c = w2_ref[cols, :].astype(BF16)
            a = [jnp.maximum(jnp.dot(h, w1c, preferred_element_type=F32), 0.0) for h in hn]
            acc = [t + jnp.dot((u * u).astype(BF16), w2c, preferred_element_type=F32) for t, u in zip(acc, a)]
        for rows, x, t in zip(subs, xs, acc):
            o_ref[rows, :] = x + _rms(t, g_ref[0, gk + 1:gk + 2, :])

    pl.when(step == 0)(functools.partial(body, True))
    pl.when(step != 0)(functools.partial(body, False))


def _mlp_layer(x, gains, layer, gk, w1, w2):
    b, s, d = x.shape
    tm = TOKEN_TILE
    rows = b * s
    d_ff = w1.shape[2]
    assert rows % tm == 0 and d_ff % FF_CHUNK == 0
    tile = _nbytes((tm, d), F32)
    weights = 2 * _nbytes((d, d_ff), F32)
    out = pl.pallas_call(
        functools.partial(_mlp_kernel, gk=gk, layer=layer),
        grid=(rows // tm,),
        in_specs=[
            pl.BlockSpec((tm, d), lambda ri: (ri, 0)),
            _layer_block(gains, layer),
            pl.BlockSpec(memory_space=pl.ANY),
            pl.BlockSpec(memory_space=pl.ANY),
        ],
        out_specs=pl.BlockSpec((tm, d), lambda ri: (ri, 0)),
        out_shape=jax.ShapeDtypeStruct((rows, d), F32),
        scratch_shapes=[
            pltpu.VMEM((d, d_ff), F32),
            pltpu.VMEM((d_ff, d), F32),
            pltpu.SemaphoreType.DMA((2, d_ff // FF_CHUNK)),
        ],
        compiler_params=_compiler_params(("arbitrary",), 2 * tile, weights, 6 * tile),
        name="mlp_sublayer",
    )(x.reshape(rows, d), gains, w1, w2)
    return out.reshape(b, s, d)


def kernel(x, mem, norm_gains, mem_norm, pool_w_in, pool_w_group, pool_scale, moba_w_qkv, moba_w_o,
           xa_w_q, xa_w_kv, xa_w_o, mlp_w1, mlp_w2):
    depth = norm_gains.shape[0]
    s, d = x.shape[1], x.shape[2]
    assert norm_gains.shape[1] == N_NORMS
    rope = _rope_tables(s)
    scale = pool_scale.reshape(pool_scale.shape[0], 1, d)
    kv_all = _mem_kv(mem, mem_norm, xa_w_kv)
    for i in range(depth):
        j = i // 2
        if i % 2 == 0:
            x = _pool_layer(x, norm_gains, i, 0, pool_w_in, pool_w_group, scale, j)
        else:
            q, k2, vt, kmean = _moba_qkv(x, norm_gains, i, 0, moba_w_qkv, j, rope)
            x = _moba_attn(x, q, k2, vt, kmean, moba_w_o, j, norm_gains, i, 1)
        x = _xattn_layer(x, norm_gains, i, 2, xa_w_q, kv_all, xa_w_o)
        x = _mlp_layer(x, norm_gains, i, 4, mlp_w1, mlp_w2)
    return x
```

```python
import functools

import jax
import jax.numpy as jnp
from jax import lax
from jax.experimental import pallas as pl
from jax.experimental.pallas import tpu as pltpu

F32 = jnp.float32
BF16 = jnp.bfloat16

RMS_EPS = 1e-6
N_NORMS = 6
POOL_WINDOWS = (2, 4, 8, 16)
POOL_GW = 256
POOL_HALO = 16
MOBA_HEADS = 8
HEAD_DIM = 128
ROT_DIM = 32
ROPE_THETA = 500000.0
MOBA_BLOCK = 256
MOBA_TOPK = 3
MEM_HEADS = 4
MEM_HEAD_DIM = 256

MLP_TILE = 512
POOL_TILE = 1024
QKV_TILE = 1024
XATTN_TILE = 1024
ATTN_QBLOCKS = 2
SCORE_LOOKAHEAD = 2
MEMKV_BATCH = 2
XATTN_SUB_ROWS = 256
MLP_SUB_ROWS = 256
FF_CHUNK = 1024
V7X_VMEM_BYTES = 64 * 1024 * 1024
V7X_VMEM_USABLE = 56 * 1024 * 1024

GATE_ROWS = 16
MASK_BIAS = -1e30
SUM_ROWS = 16
LOG2_E = 1.4426950408889634

NT_DIMS = (((1,), (1,)), ((), ()))
TN_DIMS = (((0,), (0,)), ((), ()))


def _nbytes(shape, dtype):
    n = 1
    for d in shape:
        n *= d
    return n * jnp.dtype(dtype).itemsize


def _compiler_params(semantics, pipelined_bytes, resident_bytes, temp_bytes):
    need = 2 * pipelined_bytes + resident_bytes + temp_bytes
    return pltpu.CompilerParams(
        dimension_semantics=semantics,
        vmem_limit_bytes=int(min(max(need, 16 * 1024 * 1024), V7X_VMEM_USABLE)),
    )


def _layer_block(stack, layer, **kwargs):
    shape = (1,) + tuple(stack.shape[1:])
    zeros = (0,) * (stack.ndim - 1)
    return pl.BlockSpec(shape, lambda *_: (layer,) + zeros, **kwargs)


def _sub_tiles(rows, sub_rows):
    assert rows % sub_rows == 0
    return [slice(r, r + sub_rows) for r in range(0, rows, sub_rows)]


def _rms(xf, gain):
    ms = jnp.mean(xf * xf, axis=-1, keepdims=True)
    return xf * lax.rsqrt(ms + RMS_EPS) * gain


def _pool_kernel(x_ref, g_ref, win_ref, wg_ref, scale_ref, o_ref, carry_ref, *, tm, gk):
    s = pl.program_id(1)

    @pl.when(s == 0)
    def _():
        carry_ref[...] = jnp.zeros_like(carry_ref)

    x = x_ref[0]
    hn = _rms(x, g_ref[0, gk:gk + 1, :]).astype(BF16)
    u = jnp.dot(hn, win_ref[0].astype(BF16), preferred_element_type=F32)
    ext = jnp.concatenate([carry_ref[...], u], axis=0)
    carry_ref[...] = u[tm - POOL_HALO:, :]

    pos = s * tm + lax.broadcasted_iota(jnp.int32, (tm, 1), 0)
    ys = []
    for g, w in enumerate(POOL_WINDOWS):
        cols = slice(g * POOL_GW, (g + 1) * POOL_GW)
        t = ext[:, cols]
        shift = 1
        while shift < w:
            t = t + pltpu.roll(t, shift, 0)
            shift *= 2
        win_sum = t[POOL_HALO:, :]
        cnt = jnp.minimum(pos + 1, w).astype(F32)
        pooled = win_sum / cnt - u[:, cols]
        ys.append(jnp.dot(pooled.astype(BF16), wg_ref[0, g].astype(BF16), preferred_element_type=F32))
    y = jnp.concatenate(ys, axis=1) * scale_ref[0]
    o_ref[0] = x + _rms(y, g_ref[0, gk + 1:gk + 2, :])


def _pool_layer(x, gains, layer, gk, w_in, w_group, scale, j):
    b, s, d = x.shape
    tm = POOL_TILE
    assert s % tm == 0 and tm % 8 == 0 and tm >= POOL_HALO
    tile = _nbytes((tm, d), F32)
    weights = _nbytes(w_in.shape[1:], F32) + _nbytes(w_group.shape[1:], F32)
    resident = pl.Buffered(1)
    return pl.pallas_call(
        functools.partial(_pool_kernel, tm=tm, gk=gk),
        grid=(b, s // tm),
        in_specs=[
            pl.BlockSpec((1, tm, d), lambda bi, si: (bi, si, 0)),
            _layer_block(gains, layer),
            _layer_block(w_in, j, pipeline_mode=resident),
            _layer_block(w_group, j, pipeline_mode=resident),
            _layer_block(scale, j),
        ],
        out_specs=pl.BlockSpec((1, tm, d), lambda bi, si: (bi, si, 0)),
        out_shape=jax.ShapeDtypeStruct(x.shape, F32),
        scratch_shapes=[pltpu.VMEM((POOL_HALO, d), F32)],
        compiler_params=_compiler_params(("arbitrary", "arbitrary"), 2 * tile, weights, 8 * tile),
        name="pool_sublayer",
    )(x, gains, w_in, w_group, scale)


def _rope_tables(s):
    half = ROT_DIM // 2
    pos = jnp.arange(s, dtype=F32)
    inv_freq = ROPE_THETA ** (-jnp.arange(0, ROT_DIM, 2, dtype=F32) / ROT_DIM)
    ang = pos[:, None] * inv_freq[None, :]
    cos, sin = jnp.cos(ang), jnp.sin(ang)
    pad = HEAD_DIM - ROT_DIM
    c = jnp.concatenate([cos, cos, jnp.ones((s, pad), F32)], axis=1)
    s_lo = jnp.concatenate([-sin, jnp.zeros((s, HEAD_DIM - half), F32)], axis=1)
    s_hi = jnp.concatenate([jnp.zeros((s, half), F32), sin, jnp.zeros((s, pad), F32)], axis=1)
    k_tab = jnp.stack([c, s_lo, s_hi])
    return jnp.concatenate([k_tab * (LOG2_E * HEAD_DIM ** -0.5), k_tab], axis=0)


def _qkv_kernel(x_ref, g_ref, wq_ref, wk_ref, wv_ref, rope_ref, q_ref, k2_ref, vt_ref, kmean_ref, *, tm, gk):
    s = pl.program_id(1)
    half = ROT_DIM // 2
    blocks_per_tile = tm // MOBA_BLOCK
    hn = _rms(x_ref[0], g_ref[0, gk:gk + 1, :]).astype(BF16)
    for c in range(2 * MOBA_HEADS):
        if c % 2 == 0:
            w_ref = wq_ref if c < MOBA_HEADS else wk_ref
            c0 = c % MOBA_HEADS
            w2 = w_ref[0, :, c0 * HEAD_DIM:(c0 + 2) * HEAD_DIM].astype(BF16)
            y2 = jnp.dot(hn, w2, preferred_element_type=F32)
        y = y2[:, (c % 2) * HEAD_DIM:(c % 2 + 1) * HEAD_DIM]
        t = 0 if c < MOBA_HEADS else 3
        y = (y * rope_ref[t]
             + pltpu.roll(y, HEAD_DIM - half, 1) * rope_ref[t + 1]
             + pltpu.roll(y, half, 1) * rope_ref[t + 2])
        if c < MOBA_HEADS:
            q_ref[0, c] = y.astype(BF16)
        else:
            h = c - MOBA_HEADS
            for r in range(blocks_per_tile):
                row = jnp.mean(y[r * MOBA_BLOCK:(r + 1) * MOBA_BLOCK, :], axis=0, keepdims=True)
                kmean_ref[0, h, pl.ds(s * blocks_per_tile + r, 1), :] = row
            if h % 2 == 0:
                k_even = y
            else:
                k2_ref[0, h // 2] = jnp.concatenate([k_even, y], axis=1).astype(BF16)
    vt = lax.dot_general(wv_ref[0].astype(BF16), hn, (((0,), (1,)), ((), ())), preferred_element_type=F32)
    vt_ref[0] = vt.reshape(MOBA_HEADS, HEAD_DIM, tm).astype(BF16)


def _moba_qkv(x, gains, layer, gk, w_qkv, j, rope):
    b, s, d = x.shape
    tm = QKV_TILE
    nb = s // MOBA_BLOCK
    h = MOBA_HEADS
    assert s % tm == 0 and tm % MOBA_BLOCK == 0 and d == h * HEAD_DIM and w_qkv.shape[1:] == (d, 3 * d)
    tile = _nbytes((tm, d), F32)
    out_tile = _nbytes((3 * h, tm, HEAD_DIM), BF16)
    rope_tile = _nbytes((6, tm, HEAD_DIM), F32)
    weights = _nbytes((d, 3 * d), F32)
    w_part = lambda part: pl.BlockSpec((1, d, d), lambda bi, si: (j, 0, part), pipeline_mode=pl.Buffered(1))
    return pl.pallas_call(
        functools.partial(_qkv_kernel, tm=tm, gk=gk),
        grid=(b, s // tm),
        in_specs=[
            pl.BlockSpec((1, tm, d), lambda bi, si: (bi, si, 0)),
            _layer_block(gains, layer),
            w_part(0), w_part(1), w_part(2),
            pl.BlockSpec((6, tm, HEAD_DIM), lambda bi, si: (0, si, 0)),
        ],
        out_specs=[
            pl.BlockSpec((1, h, tm, HEAD_DIM), lambda bi, si: (bi, 0, si, 0)),
            pl.BlockSpec((1, h // 2, tm, 2 * HEAD_DIM), lambda bi, si: (bi, 0, si, 0)),
            pl.BlockSpec((1, h, HEAD_DIM, tm), lambda bi, si: (bi, 0, 0, si)),
            pl.BlockSpec((1, h, nb, HEAD_DIM), lambda bi, si: (bi, 0, 0, 0)),
        ],
        out_shape=[
            jax.ShapeDtypeStruct((b, h, s, HEAD_DIM), BF16),
            jax.ShapeDtypeStruct((b, h // 2, s, 2 * HEAD_DIM), BF16),
            jax.ShapeDtypeStruct((b, h, HEAD_DIM, s), BF16),
            jax.ShapeDtypeStruct((b, h, nb, HEAD_DIM), F32),
        ],
        compiler_params=_compiler_params(
            ("arbitrary", "arbitrary"), tile + out_tile + rope_tile, weights, 4 * tile),
        name="moba_qkv",
    )(x, gains, w_qkv, w_qkv, w_qkv, rope)


def _attn_query_block(q_ref, k2_ref, vt_ref, km_ref, ot_ref, ii, qb):
    blk = MOBA_BLOCK
    hd = HEAD_DIM
    nb = k2_ref.shape[2] // blk
    nbp = GATE_ROWS
    gated = ii > MOBA_TOPK
    q_rows = slice(qb * blk, (qb + 1) * blk)

    key_id = lax.broadcasted_iota(jnp.int32, (blk, 2 * blk), 0)
    qry_id = lax.broadcasted_iota(jnp.int32, (blk, 2 * blk), 1) & (blk - 1)
    causal = key_id <= qry_id
    blk_id = lax.broadcasted_iota(jnp.int32, (nbp, blk), 0)
    zeros_q = jnp.zeros((blk, hd), BF16)
    ones = jnp.ones((SUM_ROWS, blk), BF16)

    def block_bias(q, h):
        km = km_ref[0, h]
        km = jnp.concatenate([km, jnp.zeros((nbp - nb, hd), F32)], axis=0)
        km_hi = km.astype(BF16)
        km_lo = (km - km_hi.astype(F32)).astype(BF16)
        gate = (lax.dot_general(km_hi, q, NT_DIMS, preferred_element_type=F32)
                + lax.dot_general(km_lo, q, NT_DIMS, preferred_element_type=F32))
        rank = jnp.zeros((nbp, blk), F32)
        for jp in range(ii):
            gj = gate[jp:jp + 1, :]
            beats = (gj > gate) | ((gj == gate) & (blk_id > jp))
            rank = rank + beats.astype(F32)
        return jnp.where(rank >= float(MOBA_TOPK), MASK_BIAS, 0.0)

    def pair(g, carry):
        q0 = q_ref[0, 2 * g, q_rows, :]
        q1 = q_ref[0, 2 * g + 1, q_rows, :]
        qd = jnp.concatenate([jnp.concatenate([q0, zeros_q], axis=1),
                              jnp.concatenate([zeros_q, q1], axis=1)], axis=0)
        if gated:
            bias = jnp.concatenate([block_bias(q0, 2 * g), block_bias(q1, 2 * g + 1)], axis=1)
        m = None
        acc = [None, None]
        order = [ii] + list(range(ii))

        def scores(j):
            sc = lax.dot_general(k2_ref[0, g, j * blk:(j + 1) * blk, :], qd, NT_DIMS,
                                 preferred_element_type=F32)
            if j == ii:
                sc = jnp.where(causal, sc, -jnp.inf)
            bm = jnp.max(sc, axis=0, keepdims=True)
            if gated and j < ii:
                bm = bm + bias[j:j + 1, :]
            return sc, bm

        def accumulate(j, p, alpha):
            for hh in range(2):
                cols = slice(hh * blk, (hh + 1) * blk)
                vt1 = jnp.concatenate([vt_ref[0, 2 * g + hh, :, j * blk:(j + 1) * blk], ones], axis=0)
                pv = jnp.dot(vt1, p[:, cols], preferred_element_type=F32)
                acc[hh] = pv if alpha is None else acc[hh] * alpha[:, cols] + pv

        n_blk = len(order)
        sc_q = [scores(j) for j in order[:SCORE_LOOKAHEAD]]
        pending = None
        for t, j in enumerate(order):
            sc, bm = sc_q.pop(0)
            if t + SCORE_LOOKAHEAD < n_blk:
                sc_q.append(scores(order[t + SCORE_LOOKAHEAD]))
            if pending is not None:
                accumulate(*pending)
            masked = gated and j < ii
            m_new = bm if m is None else jnp.maximum(m, bm)
            shift = bias[j:j + 1, :] - m_new if masked else -m_new
            p = jnp.exp2(sc + shift).astype(BF16)
            alpha = None if m is None else jnp.exp2(m - m_new)
            pending = (j, p, alpha)
            m = m_new
        accumulate(*pending)
        for hh in range(2):
            ot_ref[qb, 2 * g + hh] = (acc[hh][:hd, :] / acc[hh][hd:hd + 1, :]).astype(BF16)
        return carry

    lax.fori_loop(0, MOBA_HEADS // 2, pair, 0, unroll=True)


def _attn_kernel(q_ref, k2_ref, vt_ref, km_ref, x_ref, wo_ref, g_ref, o_ref, ot_ref, *, gk):
    i = pl.program_id(1)
    blk = MOBA_BLOCK
    nb = k2_ref.shape[2] // blk
    d = x_ref.shape[2]

    def grid_step(step):
        for qb in range(ATTN_QBLOCKS):
            _attn_query_block(q_ref, k2_ref, vt_ref, km_ref, ot_ref, step * ATTN_QBLOCKS + qb, qb)
        o_t = jnp.concatenate([ot_ref[qb].reshape(d, blk) for qb in range(ATTN_QBLOCKS)], axis=1)
        y = lax.dot_general(o_t, wo_ref[0].astype(BF16), TN_DIMS, preferred_element_type=F32)
        o_ref[0] = x_ref[0] + _rms(y, g_ref[0, gk:gk + 1, :])

    for step in range(nb // ATTN_QBLOCKS):
        pl.when(i == step)(functools.partial(grid_step, step))


def _moba_attn(x, q, k2, vt, kmean, w_o, j, gains, layer, gk):
    b, s, d = x.shape
    blk = MOBA_BLOCK
    nb = s // blk
    h = MOBA_HEADS
    rows = ATTN_QBLOCKS * blk
    assert nb <= GATE_ROWS and blk == 256 and h % 2 == 0 and nb % ATTN_QBLOCKS == 0
    q_tile = _nbytes((h, rows, HEAD_DIM), BF16)
    kv_tile = _nbytes((h, s, HEAD_DIM), BF16)
    x_tile = _nbytes((rows, d), F32)
    return pl.pallas_call(
        functools.partial(_attn_kernel, gk=gk),
        grid=(b, nb // ATTN_QBLOCKS),
        in_specs=[
            pl.BlockSpec((1, h, rows, HEAD_DIM), lambda bi, qi: (bi, 0, qi, 0)),
            pl.BlockSpec((1, h // 2, s, 2 * HEAD_DIM), lambda bi, qi: (bi, 0, 0, 0)),
            pl.BlockSpec((1, h, HEAD_DIM, s), lambda bi, qi: (bi, 0, 0, 0)),
            pl.BlockSpec((1, h, nb, HEAD_DIM), lambda bi, qi: (bi, 0, 0, 0)),
            pl.BlockSpec((1, rows, d), lambda bi, qi: (bi, qi, 0)),
            _layer_block(w_o, j),
            _layer_block(gains, layer),
        ],
        out_specs=pl.BlockSpec((1, rows, d), lambda bi, qi: (bi, qi, 0)),
        out_shape=jax.ShapeDtypeStruct(x.shape, F32),
        scratch_shapes=[pltpu.VMEM((ATTN_QBLOCKS, h, HEAD_DIM, blk), BF16)],
        compiler_params=_compiler_params(
            ("arbitrary", "arbitrary"), q_tile + 2 * kv_tile + 2 * x_tile + _nbytes((d, d), F32),
            x_tile, 8 * _nbytes((blk, s), F32)),
        name="moba_attention",
    )(q, k2, vt, kmean, x, w_o, gains)


def _memkv_kernel(mem_ref, g_ref, w_ref, o_ref):
    nb, m, d = mem_ref.shape
    mn = _rms(mem_ref[...].reshape(nb * m, d), g_ref[0]).astype(BF16)
    kv = jnp.dot(mn, w_ref[0].astype(BF16), preferred_element_type=F32).astype(BF16)
    o_ref[0] = kv.reshape(nb, m, kv.shape[1])


def _mem_kv(mem, mem_norm, w_kv):
    depth, d, d2 = w_kv.shape
    b, m, _ = mem.shape
    nb = MEMKV_BATCH
    assert b % nb == 0 and m % 8 == 0
    return pl.pallas_call(
        _memkv_kernel,
        grid=(depth, b // nb),
        in_specs=[
            pl.BlockSpec((nb, m, d), lambda li, bi: (bi, 0, 0)),
            pl.BlockSpec((1, 1, d), lambda li, bi: (li, 0, 0)),
            pl.BlockSpec((1, d, d2), lambda li, bi: (li, 0, 0)),
        ],
        out_specs=pl.BlockSpec((1, nb, m, d2), lambda li, bi: (li, bi, 0, 0)),
        out_shape=jax.ShapeDtypeStruct((depth, b, m, d2), BF16),
        compiler_params=_compiler_params(
            ("arbitrary", "arbitrary"),
            _nbytes((nb * m, d), F32) + _nbytes((d, d2), F32) + _nbytes((nb * m, d2), BF16), 0,
            3 * _nbytes((nb * m, d2), F32)),
        name="memory_kv",
    )(mem, mem_norm.reshape(depth, 1, d), w_kv)


def _xattn_kernel(x_ref, g_ref, wq_ref, kv_ref, wo_ref, o_ref, *, gk):
    d = x_ref.shape[2]
    hd = MEM_HEAD_DIM
    subs = _sub_tiles(x_ref.shape[1], XATTN_SUB_ROWS)
    heads = range(MEM_HEADS)
    xs = [x_ref[0, rows, :] for rows in subs]
    hn = [_rms(x, g_ref[0, gk:gk + 1, :]).astype(BF16) for x in xs]
    wq = wq_ref[0].astype(BF16)
    q_scale = LOG2_E * hd ** -0.5
    q = [(jnp.dot(h, wq, preferred_element_type=F32) * q_scale).astype(BF16) for h in hn]
    sc = [[lax.dot_general(qs[:, h * hd:(h + 1) * hd], kv_ref[0, 0, :, h * hd:(h + 1) * hd], NT_DIMS,
                           preferred_element_type=F32) for h in heads] for qs in q]
    p, inv_l = [], []
    for sc_s in sc:
        e = [jnp.exp2(t - jnp.max(t, axis=1, keepdims=True)) for t in sc_s]
        inv_l.append([1.0 / jnp.sum(t, axis=1, keepdims=True) for t in e])
        p.append([t.astype(BF16) for t in e])
    wo = wo_ref[0].astype(BF16)
    for rows, x, p_s, inv_s in zip(subs, xs, p, inv_l):
        outs = [jnp.dot(p_s[h], kv_ref[0, 0, :, d + h * hd:d + (h + 1) * hd], preferred_element_type=F32) * inv_s[h]
                for h in heads]
        o = jnp.concatenate(outs, axis=1).astype(BF16)
        y = jnp.dot(o, wo, preferred_element_type=F32)
        o_ref[0, rows, :] = x + _rms(y, g_ref[0, gk + 1:gk + 2, :])


def _xattn_layer(x, gains, layer, gk, w_q, kv_all, w_o):
    b, s, d = x.shape
    tm = XATTN_TILE
    m = kv_all.shape[2]
    assert s % tm == 0 and d == MEM_HEADS * MEM_HEAD_DIM
    tile = _nbytes((tm, d), F32)
    return pl.pallas_call(
        functools.partial(_xattn_kernel, gk=gk),
        grid=(b, s // tm),
        in_specs=[
            pl.BlockSpec((1, tm, d), lambda bi, si: (bi, si, 0)),
            _layer_block(gains, layer),
            _layer_block(w_q, layer),
            pl.BlockSpec((1, 1, m, 2 * d), lambda bi, si: (layer, bi, 0, 0)),
            _layer_block(w_o, layer),
        ],
        out_specs=pl.BlockSpec((1, tm, d), lambda bi, si: (bi, si, 0)),
        out_shape=jax.ShapeDtypeStruct(x.shape, F32),
        compiler_params=_compiler_params(
            ("arbitrary", "arbitrary"),
            2 * tile + 2 * _nbytes((d, d), F32) + _nbytes((m, 2 * d), BF16), 0, 6 * tile),
        name="memory_xattn_sublayer",
    )(x, gains, w_q, kv_all, w_o)


def _mlp_kernel(x_ref, g_ref, w1_hbm, w2_hbm, o_ref, w1_ref, w2_ref, sem, *, gk, layer):
    d_ff = w1_ref.shape[1]
    n_chunks = d_ff // FF_CHUNK
    step = pl.program_id(0)

    def chunk_copies(c):
        ff = pl.ds(c * FF_CHUNK, FF_CHUNK)
        return (pltpu.make_async_copy(w1_hbm.at[layer, :, ff], w1_ref.at[:, ff], sem.at[0, c]),
                pltpu.make_async_copy(w2_hbm.at[layer, ff, :], w2_ref.at[ff, :], sem.at[1, c]))

    @pl.when(step == 0)
    def _():
        for c in range(n_chunks):
            for copy in chunk_copies(c):
                copy.start()

    def body(wait_for_weights):
        subs = _sub_tiles(x_ref.shape[0], MLP_SUB_ROWS)
        xs = [x_ref[rows, :] for rows in subs]
        hn = [_rms(x, g_ref[0, gk:gk + 1, :]).astype(BF16) for x in xs]
        acc = [jnp.zeros(x.shape, F32) for x in xs]
        for c in range(n_chunks):
            if wait_for_weights:
                for copy in chunk_copies(c):
                    copy.wait()
            cols = slice(c * FF_CHUNK, (c + 1) * FF_CHUNK)
            w1c = w1_ref[:, cols].astype(BF16)
            w2c = w2_ref[cols, :].astype(BF16)
            a = [jnp.maximum(jnp.dot(h, w1c, preferred_element_type=F32), 0.0) for h in hn]
            acc = [t + jnp.dot((u * u).astype(BF16), w2c, preferred_element_type=F32) for t, u in zip(acc, a)]
        for rows, x, t in zip(subs, xs, acc):
            o_ref[rows, :] = x + _rms(t, g_ref[0, gk + 1:gk + 2, :])

    pl.when(step == 0)(functools.partial(body, True))
    pl.when(step != 0)(functools.partial(body, False))


def _mlp_layer(x, gains, layer, gk, w1, w2):
    b, s, d = x.shape
    tm = MLP_TILE
    rows = b * s
    d_ff = w1.shape[2]
    assert rows % tm == 0 and d_ff % FF_CHUNK == 0
    tile = _nbytes((tm, d), F32)
    weights = 2 * _nbytes((d, d_ff), F32)
    out = pl.pallas_call(
        functools.partial(_mlp_kernel, gk=gk, layer=layer),
        grid=(rows // tm,),
        in_specs=[
            pl.BlockSpec((tm, d), lambda ri: (ri, 0)),
            _layer_block(gains, layer),
            pl.BlockSpec(memory_space=pl.ANY),
            pl.BlockSpec(memory_space=pl.ANY),
        ],
        out_specs=pl.BlockSpec((tm, d), lambda ri: (ri, 0)),
        out_shape=jax.ShapeDtypeStruct((rows, d), F32),
        scratch_shapes=[
            pltpu.VMEM((d, d_ff), F32),
            pltpu.VMEM((d_ff, d), F32),
            pltpu.SemaphoreType.DMA((2, d_ff // FF_CHUNK)),
        ],
        compiler_params=_compiler_params(("arbitrary",), 2 * tile, weights, 6 * tile),
        name="mlp_sublayer",
    )(x.reshape(rows, d), gains, w1, w2)
    return out.reshape(b, s, d)


def kernel(x, mem, norm_gains, mem_norm, pool_w_in, pool_w_group, pool_scale, moba_w_qkv, moba_w_o,
           xa_w_q, xa_w_kv, xa_w_o, mlp_w1, mlp_w2):
    depth = norm_gains.shape[0]
    s, d = x.shape[1], x.shape[2]
    assert norm_gains.shape[1] == N_NORMS
    rope = _rope_tables(s)
    scale = pool_scale.reshape(pool_scale.shape[0], 1, d)
    kv_all = _mem_kv(mem, mem_norm, xa_w_kv)
    for i in range(depth):
        j = i // 2
        if i % 2 == 0:
            x = _pool_layer(x, norm_gains, i, 0, pool_w_in, pool_w_group, scale, j)
        else:
            q, k2, vt, kmean = _moba_qkv(x, norm_gains, i, 0, moba_w_qkv, j, rope)
            x = _moba_attn(x, q, k2, vt, kmean, moba_w_o, j, norm_gains, i, 1)
        x = _xattn_layer(x, norm_gains, i, 2, xa_w_q, kv_all, xa_w_o)
        x = _mlp_layer(x, norm_gains, i, 4, mlp_w1, mlp_w2)
    return x
```

```python
import functools

import jax
import jax.numpy as jnp
from jax import lax
from jax.experimental import pallas as pl
from jax.experimental.pallas import tpu as pltpu

F32 = jnp.float32
BF16 = jnp.bfloat16

RMS_EPS = 1e-6
N_NORMS = 6
POOL_WINDOWS = (2, 4, 8, 16)
POOL_GW = 256
POOL_HALO = 16
MOBA_HEADS = 8
HEAD_DIM = 128
ROT_DIM = 32
ROPE_THETA = 500000.0
MOBA_BLOCK = 256
MOBA_TOPK = 3
MEM_HEADS = 4
MEM_HEAD_DIM = 256

MLP_TILE = 512
POOL_TILE = 1024
QKV_TILE = 1024
XATTN_TILE = 1024
ATTN_QBLOCKS = 2
SCORE_LOOKAHEAD = 2
MEMKV_BATCH = 2
XATTN_SUB_ROWS = 256
MLP_SUB_ROWS = 256
FF_CHUNK = 1024
V7X_VMEM_BYTES = 64 * 1024 * 1024
V7X_VMEM_USABLE = 56 * 1024 * 1024

GATE_ROWS = 16
MASK_BIAS = -1e30
SUM_ROWS = 16
LOG2_E = 1.4426950408889634

NT_DIMS = (((1,), (1,)), ((), ()))
TN_DIMS = (((0,), (0,)), ((), ()))


def _nbytes(shape, dtype):
    n = 1
    for d in shape:
        n *= d
    return n * jnp.dtype(dtype).itemsize


def _compiler_params(semantics, pipelined_bytes, resident_bytes, temp_bytes):
    need = 2 * pipelined_bytes + resident_bytes + temp_bytes
    assert need <= V7X_VMEM_BYTES, need
    return pltpu.CompilerParams(dimension_semantics=semantics, vmem_limit_bytes=V7X_VMEM_USABLE)


def _layer_block(stack, layer, **kwargs):
    shape = (1,) + tuple(stack.shape[1:])
    zeros = (0,) * (stack.ndim - 1)
    return pl.BlockSpec(shape, lambda *_: (layer,) + zeros, **kwargs)


def _sub_tiles(rows, sub_rows):
    assert rows % sub_rows == 0
    return [slice(r, r + sub_rows) for r in range(0, rows, sub_rows)]


def _rms(xf, gain):
    ms = jnp.mean(xf * xf, axis=-1, keepdims=True)
    return xf * lax.rsqrt(ms + RMS_EPS) * gain


def _pool_kernel(x_ref, g_ref, win_ref, wg_ref, scale_ref, o_ref, carry_ref, *, tm, gk):
    s = pl.program_id(1)

    @pl.when(s == 0)
    def _():
        carry_ref[...] = jnp.zeros_like(carry_ref)

    x = x_ref[0]
    hn = _rms(x, g_ref[0, gk:gk + 1, :]).astype(BF16)
    u = jnp.dot(hn, win_ref[0].astype(BF16), preferred_element_type=F32)
    ext = jnp.concatenate([carry_ref[...], u], axis=0)
    carry_ref[...] = u[tm - POOL_HALO:, :]

    pos = s * tm + lax.broadcasted_iota(jnp.int32, (tm, 1), 0)
    ys = []
    for g, w in enumerate(POOL_WINDOWS):
        cols = slice(g * POOL_GW, (g + 1) * POOL_GW)
        t = ext[:, cols]
        shift = 1
        while shift < w:
            t = t + pltpu.roll(t, shift, 0)
            shift *= 2
        win_sum = t[POOL_HALO:, :]
        cnt = jnp.minimum(pos + 1, w).astype(F32)
        pooled = win_sum / cnt - u[:, cols]
        ys.append(jnp.dot(pooled.astype(BF16), wg_ref[0, g].astype(BF16), preferred_element_type=F32))
    y = jnp.concatenate(ys, axis=1) * scale_ref[0]
    o_ref[0] = x + _rms(y, g_ref[0, gk + 1:gk + 2, :])


def _pool_layer(x, gains, layer, gk, w_in, w_group, scale, j):
    b, s, d = x.shape
    tm = POOL_TILE
    assert s % tm == 0 and tm % 8 == 0 and tm >= POOL_HALO
    tile = _nbytes((tm, d), F32)
    weights = _nbytes(w_in.shape[1:], F32) + _nbytes(w_group.shape[1:], F32)
    resident = pl.Buffered(1)
    return pl.pallas_call(
        functools.partial(_pool_kernel, tm=tm, gk=gk),
        grid=(b, s // tm),
        in_specs=[
            pl.BlockSpec((1, tm, d), lambda bi, si: (bi, si, 0)),
            _layer_block(gains, layer),
            _layer_block(w_in, j, pipeline_mode=resident),
            _layer_block(w_group, j, pipeline_mode=resident),
            _layer_block(scale, j),
        ],
        out_specs=pl.BlockSpec((1, tm, d), lambda bi, si: (bi, si, 0)),
        out_shape=jax.ShapeDtypeStruct(x.shape, F32),
        scratch_shapes=[pltpu.VMEM((POOL_HALO, d), F32)],
        compiler_params=_compiler_params(("arbitrary", "arbitrary"), 2 * tile, weights, 8 * tile),
        name="pool_sublayer",
    )(x, gains, w_in, w_group, scale)


def _rope_tables(s):
    half = ROT_DIM // 2
    pos = jnp.arange(s, dtype=F32)
    inv_freq = ROPE_THETA ** (-jnp.arange(0, ROT_DIM, 2, dtype=F32) / ROT_DIM)
    ang = pos[:, None] * inv_freq[None, :]
    cos, sin = jnp.cos(ang), jnp.sin(ang)
    pad = HEAD_DIM - ROT_DIM
    c = jnp.concatenate([cos, cos, jnp.ones((s, pad), F32)], axis=1)
    s_lo = jnp.concatenate([-sin, jnp.zeros((s, HEAD_DIM - half), F32)], axis=1)
    s_hi = jnp.concatenate([jnp.zeros((s, half), F32), sin, jnp.zeros((s, pad), F32)], axis=1)
    k_tab = jnp.stack([c, s_lo, s_hi])
    return jnp.concatenate([k_tab * (LOG2_E * HEAD_DIM ** -0.5), k_tab], axis=0)


def _qkv_kernel(x_ref, g_ref, wq_ref, wk_ref, wv_ref, rope_ref, q_ref, k2_ref, vt_ref, kmean_ref, *, tm, gk):
    s = pl.program_id(1)
    half = ROT_DIM // 2
    blocks_per_tile = tm // MOBA_BLOCK
    hn = _rms(x_ref[0], g_ref[0, gk:gk + 1, :]).astype(BF16)
    for c in range(2 * MOBA_HEADS):
        if c % 2 == 0:
            w_ref = wq_ref if c < MOBA_HEADS else wk_ref
            c0 = c % MOBA_HEADS
            w2 = w_ref[0, :, c0 * HEAD_DIM:(c0 + 2) * HEAD_DIM].astype(BF16)
            y2 = jnp.dot(hn, w2, preferred_element_type=F32)
        y = y2[:, (c % 2) * HEAD_DIM:(c % 2 + 1) * HEAD_DIM]
        t = 0 if c < MOBA_HEADS else 3
        y = (y * rope_ref[t]
             + pltpu.roll(y, HEAD_DIM - half, 1) * rope_ref[t + 1]
             + pltpu.roll(y, half, 1) * rope_ref[t + 2])
        if c < MOBA_HEADS:
            q_ref[0, c] = y.astype(BF16)
        else:
            h = c - MOBA_HEADS
            for r in range(blocks_per_tile):
                row = jnp.mean(y[r * MOBA_BLOCK:(r + 1) * MOBA_BLOCK, :], axis=0, keepdims=True)
                kmean_ref[0, h, pl.ds(s * blocks_per_tile + r, 1), :] = row
            if h % 2 == 0:
                k_even = y
            else:
                k2_ref[0, h // 2] = jnp.concatenate([k_even, y], axis=1).astype(BF16)
    vt = lax.dot_general(wv_ref[0].astype(BF16), hn, (((0,), (1,)), ((), ())), preferred_element_type=F32)
    vt_ref[0] = vt.reshape(MOBA_HEADS, HEAD_DIM, tm).astype(BF16)


def _moba_qkv(x, gains, layer, gk, w_qkv, j, rope):
    b, s, d = x.shape
    tm = QKV_TILE
    nb = s // MOBA_BLOCK
    h = MOBA_HEADS
    assert s % tm == 0 and tm % MOBA_BLOCK == 0 and d == h * HEAD_DIM and w_qkv.shape[1:] == (d, 3 * d)
    tile = _nbytes((tm, d), F32)
    out_tile = _nbytes((3 * h, tm, HEAD_DIM), BF16)
    rope_tile = _nbytes((6, tm, HEAD_DIM), F32)
    weights = _nbytes((d, 3 * d), F32)
    w_part = lambda part: pl.BlockSpec((1, d, d), lambda bi, si: (j, 0, part), pipeline_mode=pl.Buffered(1))
    return pl.pallas_call(
        functools.partial(_qkv_kernel, tm=tm, gk=gk),
        grid=(b, s // tm),
        in_specs=[
            pl.BlockSpec((1, tm, d), lambda bi, si: (bi, si, 0)),
            _layer_block(gains, layer),
            w_part(0), w_part(1), w_part(2),
            pl.BlockSpec((6, tm, HEAD_DIM), lambda bi, si: (0, si, 0)),
        ],
        out_specs=[
            pl.BlockSpec((1, h, tm, HEAD_DIM), lambda bi, si: (bi, 0, si, 0)),
            pl.BlockSpec((1, h // 2, tm, 2 * HEAD_DIM), lambda bi, si: (bi, 0, si, 0)),
            pl.BlockSpec((1, h, HEAD_DIM, tm), lambda bi, si: (bi, 0, 0, si)),
            pl.BlockSpec((1, h, nb, HEAD_DIM), lambda bi, si: (bi, 0, 0, 0)),
        ],
        out_shape=[
            jax.ShapeDtypeStruct((b, h, s, HEAD_DIM), BF16),
            jax.ShapeDtypeStruct((b, h // 2, s, 2 * HEAD_DIM), BF16),
            jax.ShapeDtypeStruct((b, h, HEAD_DIM, s), BF16),
            jax.ShapeDtypeStruct((b, h, nb, HEAD_DIM), F32),
        ],
        compiler_params=_compiler_params(
            ("arbitrary", "arbitrary"), tile + out_tile + rope_tile, weights, 4 * tile),
        name="moba_qkv",
    )(x, gains, w_qkv, w_qkv, w_qkv, rope)


def _attn_query_block(q_ref, k2_ref, vt_ref, km_ref, ot_ref, ii, qb):
    blk = MOBA_BLOCK
    hd = HEAD_DIM
    nb = k2_ref.shape[2] // blk
    nbp = GATE_ROWS
    gated = ii > MOBA_TOPK
    q_rows = slice(qb * blk, (qb + 1) * blk)

    key_id = lax.broadcasted_iota(jnp.int32, (blk, 2 * blk), 0)
    qry_id = lax.broadcasted_iota(jnp.int32, (blk, 2 * blk), 1) & (blk - 1)
    causal = key_id <= qry_id
    blk_id = lax.broadcasted_iota(jnp.int32, (nbp, blk), 0)
    zeros_q = jnp.zeros((blk, hd), BF16)
    ones = jnp.ones((SUM_ROWS, blk), BF16)

    def block_bias(q, h):
        km = km_ref[0, h]
        km = jnp.concatenate([km, jnp.zeros((nbp - nb, hd), F32)], axis=0)
        km_hi = km.astype(BF16)
        km_lo = (km - km_hi.astype(F32)).astype(BF16)
        gate = (lax.dot_general(km_hi, q, NT_DIMS, preferred_element_type=F32)
                + lax.dot_general(km_lo, q, NT_DIMS, preferred_element_type=F32))
        rank = jnp.zeros((nbp, blk), F32)
        for jp in range(ii):
            gj = gate[jp:jp + 1, :]
            beats = (gj > gate) | ((gj == gate) & (blk_id > jp))
            rank = rank + beats.astype(F32)
        return jnp.where(rank >= float(MOBA_TOPK), MASK_BIAS, 0.0)

    def pair(g, carry):
        q0 = q_ref[0, 2 * g, q_rows, :]
        q1 = q_ref[0, 2 * g + 1, q_rows, :]
        qd = jnp.concatenate([jnp.concatenate([q0, zeros_q], axis=1),
                              jnp.concatenate([zeros_q, q1], axis=1)], axis=0)
        if gated:
            bias = jnp.concatenate([block_bias(q0, 2 * g), block_bias(q1, 2 * g + 1)], axis=1)
        m = None
        acc = [None, None]
        order = [ii] + list(range(ii))

        def scores(j):
            sc = lax.dot_general(k2_ref[0, g, j * blk:(j + 1) * blk, :], qd, NT_DIMS,
                                 preferred_element_type=F32)
            if j == ii:
                sc = jnp.where(causal, sc, -jnp.inf)
            bm = jnp.max(sc, axis=0, keepdims=True)
            if gated and j < ii:
                bm = bm + bias[j:j + 1, :]
            return sc, bm

        def accumulate(j, p, alpha):
            for hh in range(2):
                cols = slice(hh * blk, (hh + 1) * blk)
                vt1 = jnp.concatenate([vt_ref[0, 2 * g + hh, :, j * blk:(j + 1) * blk], ones], axis=0)
                pv = jnp.dot(vt1, p[:, cols], preferred_element_type=F32)
                acc[hh] = pv if alpha is None else acc[hh] * alpha[:, cols] + pv

        n_blk = len(order)
        sc_q = [scores(j) for j in order[:SCORE_LOOKAHEAD]]
        pending = None
        for t, j in enumerate(order):
            sc, bm = sc_q.pop(0)
            if t + SCORE_LOOKAHEAD < n_blk:
                sc_q.append(scores(order[t + SCORE_LOOKAHEAD]))
            if pending is not None:
                accumulate(*pending)
            masked = gated and j < ii
            m_new = bm if m is None else jnp.maximum(m, bm)
            shift = bias[j:j + 1, :] - m_new if masked else -m_new
            p = jnp.exp2(sc + shift).astype(BF16)
            alpha = None if m is None else jnp.exp2(m - m_new)
            pending = (j, p, alpha)
            m = m_new
        accumulate(*pending)
        for hh in range(2):
            ot_ref[qb, 2 * g + hh] = (acc[hh][:hd, :] / acc[hh][hd:hd + 1, :]).astype(BF16)
        return carry

    lax.fori_loop(0, MOBA_HEADS // 2, pair, 0, unroll=True)


def _attn_kernel(q_ref, k2_ref, vt_ref, km_ref, x_ref, wo_ref, g_ref, o_ref, ot_ref, *, gk):
    i = pl.program_id(1)
    blk = MOBA_BLOCK
    nb = k2_ref.shape[2] // blk
    d = x_ref.shape[2]

    def grid_step(step):
        for qb in range(ATTN_QBLOCKS):
            _attn_query_block(q_ref, k2_ref, vt_ref, km_ref, ot_ref, step * ATTN_QBLOCKS + qb, qb)
        o_t = jnp.concatenate([ot_ref[qb].reshape(d, blk) for qb in range(ATTN_QBLOCKS)], axis=1)
        y = lax.dot_general(o_t, wo_ref[0].astype(BF16), TN_DIMS, preferred_element_type=F32)
        o_ref[0] = x_ref[0] + _rms(y, g_ref[0, gk:gk + 1, :])

    for step in range(nb // ATTN_QBLOCKS):
        pl.when(i == step)(functools.partial(grid_step, step))


def _moba_attn(x, q, k2, vt, kmean, w_o, j, gains, layer, gk):
    b, s, d = x.shape
    blk = MOBA_BLOCK
    nb = s // blk
    h = MOBA_HEADS
    rows = ATTN_QBLOCKS * blk
    assert nb <= GATE_ROWS and blk == 256 and h % 2 == 0 and nb % ATTN_QBLOCKS == 0
    q_tile = _nbytes((h, rows, HEAD_DIM), BF16)
    kv_tile = _nbytes((h, s, HEAD_DIM), BF16)
    x_tile = _nbytes((rows, d), F32)
    return pl.pallas_call(
        functools.partial(_attn_kernel, gk=gk),
        grid=(b, nb // ATTN_QBLOCKS),
        in_specs=[
            pl.BlockSpec((1, h, rows, HEAD_DIM), lambda bi, qi: (bi, 0, qi, 0)),
            pl.BlockSpec((1, h // 2, s, 2 * HEAD_DIM), lambda bi, qi: (bi, 0, 0, 0)),
            pl.BlockSpec((1, h, HEAD_DIM, s), lambda bi, qi: (bi, 0, 0, 0)),
            pl.BlockSpec((1, h, nb, HEAD_DIM), lambda bi, qi: (bi, 0, 0, 0)),
            pl.BlockSpec((1, rows, d), lambda bi, qi: (bi, qi, 0)),
            _layer_block(w_o, j),
            _layer_block(gains, layer),
        ],
        out_specs=pl.BlockSpec((1, rows, d), lambda bi, qi: (bi, qi, 0)),
        out_shape=jax.ShapeDtypeStruct(x.shape, F32),
        scratch_shapes=[pltpu.VMEM((ATTN_QBLOCKS, h, HEAD_DIM, blk), BF16)],
        compiler_params=_compiler_params(
            ("arbitrary", "arbitrary"), q_tile + 2 * kv_tile + 2 * x_tile + _nbytes((d, d), F32),
            x_tile, 8 * _nbytes((blk, s), F32)),
        name="moba_attention",
    )(q, k2, vt, kmean, x, w_o, gains)


def _memkv_kernel(mem_ref, g_ref, w_ref, o_ref):
    nb, m, d = mem_ref.shape
    mn = _rms(mem_ref[...].reshape(nb * m, d), g_ref[0]).astype(BF16)
    kv = jnp.dot(mn, w_ref[0].astype(BF16), preferred_element_type=F32).astype(BF16)
    o_ref[0] = kv.reshape(nb, m, kv.shape[1])


def _mem_kv(mem, mem_norm, w_kv):
    depth, d, d2 = w_kv.shape
    b, m, _ = mem.shape
    nb = MEMKV_BATCH
    assert b % nb == 0 and m % 8 == 0
    return pl.pallas_call(
        _memkv_kernel,
        grid=(depth, b // nb),
        in_specs=[
            pl.BlockSpec((nb, m, d), lambda li, bi: (bi, 0, 0)),
            pl.BlockSpec((1, 1, d), lambda li, bi: (li, 0, 0)),
            pl.BlockSpec((1, d, d2), lambda li, bi: (li, 0, 0)),
        ],
        out_specs=pl.BlockSpec((1, nb, m, d2), lambda li, bi: (li, bi, 0, 0)),
        out_shape=jax.ShapeDtypeStruct((depth, b, m, d2), BF16),
        compiler_params=_compiler_params(
            ("arbitrary", "arbitrary"),
            _nbytes((nb * m, d), F32) + _nbytes((d, d2), F32) + _nbytes((nb * m, d2), BF16), 0,
            3 * _nbytes((nb * m, d2), F32)),
        name="memory_kv",
    )(mem, mem_norm.reshape(depth, 1, d), w_kv)


def _xattn_kernel(x_ref, g_ref, wq_ref, kv_ref, wo_ref, o_ref, *, gk):
    d = x_ref.shape[2]
    hd = MEM_HEAD_DIM
    subs = _sub_tiles(x_ref.shape[1], XATTN_SUB_ROWS)
    heads = range(MEM_HEADS)
    xs = [x_ref[0, rows, :] for rows in subs]
    hn = [_rms(x, g_ref[0, gk:gk + 1, :]).astype(BF16) for x in xs]
    wq = wq_ref[0].astype(BF16)
    q_scale = LOG2_E * hd ** -0.5
    q = [(jnp.dot(h, wq, preferred_element_type=F32) * q_scale).astype(BF16) for h in hn]
    sc = [[lax.dot_general(qs[:, h * hd:(h + 1) * hd], kv_ref[0, 0, :, h * hd:(h + 1) * hd], NT_DIMS,
                           preferred_element_type=F32) for h in heads] for qs in q]
    p, inv_l = [], []
    for sc_s in sc:
        e = [jnp.exp2(t - jnp.max(t, axis=1, keepdims=True)) for t in sc_s]
        inv_l.append([1.0 / jnp.sum(t, axis=1, keepdims=True) for t in e])
        p.append([t.astype(BF16) for t in e])
    wo = wo_ref[0].astype(BF16)
    for rows, x, p_s, inv_s in zip(subs, xs, p, inv_l):
        outs = [jnp.dot(p_s[h], kv_ref[0, 0, :, d + h * hd:d + (h + 1) * hd], preferred_element_type=F32) * inv_s[h]
                for h in heads]
        o = jnp.concatenate(outs, axis=1).astype(BF16)
        y = jnp.dot(o, wo, preferred_element_type=F32)
        o_ref[0, rows, :] = x + _rms(y, g_ref[0, gk + 1:gk + 2, :])


def _xattn_layer(x, gains, layer, gk, w_q, kv_all, w_o):
    b, s, d = x.shape
    tm = XATTN_TILE
    m = kv_all.shape[2]
    assert s % tm == 0 and d == MEM_HEADS * MEM_HEAD_DIM
    tile = _nbytes((tm, d), F32)
    return pl.pallas_call(
        functools.partial(_xattn_kernel, gk=gk),
        grid=(b, s // tm),
        in_specs=[
            pl.BlockSpec((1, tm, d), lambda bi, si: (bi, si, 0)),
            _layer_block(gains, layer),
            _layer_block(w_q, layer),
            pl.BlockSpec((1, 1, m, 2 * d), lambda bi, si: (layer, bi, 0, 0)),
            _layer_block(w_o, layer),
        ],
        out_specs=pl.BlockSpec((1, tm, d), lambda bi, si: (bi, si, 0)),
        out_shape=jax.ShapeDtypeStruct(x.shape, F32),
        compiler_params=_compiler_params(
            ("arbitrary", "arbitrary"),
            2 * tile + 2 * _nbytes((d, d), F32) + _nbytes((m, 2 * d), BF16), 0, 6 * tile),
        name="memory_xattn_sublayer",
    )(x, gains, w_q, kv_all, w_o)


def _mlp_kernel(x_ref, g_ref, w1_hbm, w2_hbm, o_ref, w1_ref, w2_ref, sem, *, gk, layer):
    d_ff = w1_ref.shape[1]
    n_chunks = d_ff // FF_CHUNK
    step = pl.program_id(0)

    def chunk_copies(c):
        ff = pl.ds(c * FF_CHUNK, FF_CHUNK)
        return (pltpu.make_async_copy(w1_hbm.at[layer, :, ff], w1_ref.at[:, ff], sem.at[0, c]),
                pltpu.make_async_copy(w2_hbm.at[layer, ff, :], w2_ref.at[ff, :], sem.at[1, c]))

    @pl.when(step == 0)
    def _():
        for c in range(n_chunks):
            for copy in chunk_copies(c):
                copy.start()

    def body(wait_for_weights):
        subs = _sub_tiles(x_ref.shape[0], MLP_SUB_ROWS)
        xs = [x_ref[rows, :] for rows in subs]
        hn = [_rms(x, g_ref[0, gk:gk + 1, :]).astype(BF16) for x in xs]
        acc = [jnp.zeros(x.shape, F32) for x in xs]
        for c in range(n_chunks):
            if wait_for_weights:
                for copy in chunk_copies(c):
                    copy.wait()
            cols = slice(c * FF_CHUNK, (c + 1) * FF_CHUNK)
            w1c = w1_ref[:, cols].astype(BF16)
            w2c = w2_ref[cols, :].astype(BF16)
            a = [jnp.maximum(jnp.dot(h, w1c, preferred_element_type=F32), 0.0) for h in hn]
            acc = [t + jnp.dot((u * u).astype(BF16), w2c, preferred_element_type=F32) for t, u in zip(acc, a)]
        for rows, x, t in zip(subs, xs, acc):
            o_ref[rows, :] = x + _rms(t, g_ref[0, gk + 1:gk + 2, :])

    pl.when(step == 0)(functools.partial(body, True))
    pl.when(step != 0)(functools.partial(body, False))


def _mlp_layer(x, gains, layer, gk, w1, w2):
    b, s, d = x.shape
    tm = MLP_TILE
    rows = b * s
    d_ff = w1.shape[2]
    assert rows % tm == 0 and d_ff % FF_CHUNK == 0
    tile = _nbytes((tm, d), F32)
    weights = 2 * _nbytes((d, d_ff), F32)
    out = pl.pallas_call(
        functools.partial(_mlp_kernel, gk=gk, layer=layer),
        grid=(rows // tm,),
        in_specs=[
            pl.BlockSpec((tm, d), lambda ri: (ri, 0)),
            _layer_block(gains, layer),
            pl.BlockSpec(memory_space=pl.ANY),
            pl.BlockSpec(memory_space=pl.ANY),
        ],
        out_specs=pl.BlockSpec((tm, d), lambda ri: (ri, 0)),
        out_shape=jax.ShapeDtypeStruct((rows, d), F32),
        scratch_shapes=[
            pltpu.VMEM((d, d_ff), F32),
            pltpu.VMEM((d_ff, d), F32),
            pltpu.SemaphoreType.DMA((2, d_ff // FF_CHUNK)),
        ],
        compiler_params=_compiler_params(("arbitrary",), 2 * tile, weights, 6 * tile),
        name="mlp_sublayer",
    )(x.reshape(rows, d), gains, w1, w2)
    return out.reshape(b, s, d)


def kernel(x, mem, norm_gains, mem_norm, pool_w_in, pool_w_group, pool_scale, moba_w_qkv, moba_w_o,
           xa_w_q, xa_w_kv, xa_w_o, mlp_w1, mlp_w2):
    depth = norm_gains.shape[0]
    s, d = x.shape[1], x.shape[2]
    assert norm_gains.shape[1] == N_NORMS
    rope = _rope_tables(s)
    scale = pool_scale.reshape(pool_scale.shape[0], 1, d)
    kv_all = _mem_kv(mem, mem_norm, xa_w_kv)
    for i in range(depth):
        j = i // 2
        if i % 2 == 0:
            x = _pool_layer(x, norm_gains, i, 0, pool_w_in, pool_w_group, scale, j)
        else:
            q, k2, vt, kmean = _moba_qkv(x, norm_gains, i, 0, moba_w_qkv, j, rope)
            x = _moba_attn(x, q, k2, vt, kmean, moba_w_o, j, norm_gains, i, 1)
        x = _xattn_layer(x, norm_gains, i, 2, xa_w_q, kv_all, xa_w_o)
        x = _mlp_layer(x, norm_gains, i, 4, mlp_w1, mlp_w2)
    return x
```

```python
import functools

import jax
import jax.numpy as jnp
from jax import lax
from jax.experimental import pallas as pl
from jax.experimental.pallas import tpu as pltpu

F32 = jnp.float32
BF16 = jnp.bfloat16

RMS_EPS = 1e-6
N_NORMS = 6
POOL_WINDOWS = (2, 4, 8, 16)
POOL_GW = 256
POOL_HALO = 16
MOBA_HEADS = 8
HEAD_DIM = 128
ROT_DIM = 32
ROPE_THETA = 500000.0
MOBA_BLOCK = 256
MOBA_TOPK = 3
MEM_HEADS = 4
MEM_HEAD_DIM = 256

MLP_TILE = 512
POOL_TILE = 512
QKV_TILE = 512
XATTN_TILE = 1024
ATTN_QBLOCKS = 2
SCORE_LOOKAHEAD = 2
MEMKV_BATCH = 2
XATTN_SUB_ROWS = 256
MLP_SUB_ROWS = 256
FF_CHUNK = 1024
V7X_VMEM_BYTES = 64 * 1024 * 1024
V7X_VMEM_USABLE = 56 * 1024 * 1024

GATE_ROWS = 16
MASK_BIAS = -1e30
SUM_ROWS = 16
LOG2_E = 1.4426950408889634

NT_DIMS = (((1,), (1,)), ((), ()))
TN_DIMS = (((0,), (0,)), ((), ()))


def _nbytes(shape, dtype):
    n = 1
    for d in shape:
        n *= d
    return n * jnp.dtype(dtype).itemsize


def _compiler_params(semantics, pipelined_bytes, resident_bytes, temp_bytes):
    need = 2 * pipelined_bytes + resident_bytes + temp_bytes
    return pltpu.CompilerParams(
        dimension_semantics=semantics,
        vmem_limit_bytes=int(min(max(need, 16 * 1024 * 1024), V7X_VMEM_USABLE)),
    )


def _layer_block(stack, layer, **kwargs):
    shape = (1,) + tuple(stack.shape[1:])
    zeros = (0,) * (stack.ndim - 1)
    return pl.BlockSpec(shape, lambda *_: (layer,) + zeros, **kwargs)


def _sub_tiles(rows, sub_rows):
    assert rows % sub_rows == 0
    return [slice(r, r + sub_rows) for r in range(0, rows, sub_rows)]


def _rms(xf, gain):
    ms = jnp.mean(xf * xf, axis=-1, keepdims=True)
    return xf * lax.rsqrt(ms + RMS_EPS) * gain


def _pool_kernel(x_ref, g_ref, win_ref, wg_ref, scale_ref, o_ref, carry_ref, *, tm, gk):
    s = pl.program_id(1)

    @pl.when(s == 0)
    def _():
        carry_ref[...] = jnp.zeros_like(carry_ref)

    x = x_ref[0]
    hn = _rms(x, g_ref[0, gk:gk + 1, :]).astype(BF16)
    u = jnp.dot(hn, win_ref[0].astype(BF16), preferred_element_type=F32)
    ext = jnp.concatenate([carry_ref[...], u], axis=0)
    carry_ref[...] = u[tm - POOL_HALO:, :]

    pos = s * tm + lax.broadcasted_iota(jnp.int32, (tm, 1), 0)
    ys = []
    for g, w in enumerate(POOL_WINDOWS):
        cols = slice(g * POOL_GW, (g + 1) * POOL_GW)
        t = ext[:, cols]
        shift = 1
        while shift < w:
            t = t + pltpu.roll(t, shift, 0)
            shift *= 2
        win_sum = t[POOL_HALO:, :]
        cnt = jnp.minimum(pos + 1, w).astype(F32)
        pooled = win_sum / cnt - u[:, cols]
        ys.append(jnp.dot(pooled.astype(BF16), wg_ref[0, g].astype(BF16), preferred_element_type=F32))
    y = jnp.concatenate(ys, axis=1) * scale_ref[0]
    o_ref[0] = x + _rms(y, g_ref[0, gk + 1:gk + 2, :])


def _pool_layer(x, gains, layer, gk, w_in, w_group, scale, j):
    b, s, d = x.shape
    tm = POOL_TILE
    assert s % tm == 0 and tm % 8 == 0 and tm >= POOL_HALO
    tile = _nbytes((tm, d), F32)
    weights = _nbytes(w_in.shape[1:], F32) + _nbytes(w_group.shape[1:], F32)
    return pl.pallas_call(
        functools.partial(_pool_kernel, tm=tm, gk=gk),
        grid=(b, s // tm),
        in_specs=[
            pl.BlockSpec((1, tm, d), lambda bi, si: (bi, si, 0)),
            _layer_block(gains, layer),
            _layer_block(w_in, j),
            _layer_block(w_group, j),
            _layer_block(scale, j),
        ],
        out_specs=pl.BlockSpec((1, tm, d), lambda bi, si: (bi, si, 0)),
        out_shape=jax.ShapeDtypeStruct(x.shape, F32),
        scratch_shapes=[pltpu.VMEM((POOL_HALO, d), F32)],
        compiler_params=_compiler_params(("arbitrary", "arbitrary"), 2 * tile + weights, 0, 8 * tile),
        name="pool_sublayer",
    )(x, gains, w_in, w_group, scale)


def _rope_tables(s):
    half = ROT_DIM // 2
    pos = jnp.arange(s, dtype=F32)
    inv_freq = ROPE_THETA ** (-jnp.arange(0, ROT_DIM, 2, dtype=F32) / ROT_DIM)
    ang = pos[:, None] * inv_freq[None, :]
    cos, sin = jnp.cos(ang), jnp.sin(ang)
    pad = HEAD_DIM - ROT_DIM
    c = jnp.concatenate([cos, cos, jnp.ones((s, pad), F32)], axis=1)
    s_lo = jnp.concatenate([-sin, jnp.zeros((s, HEAD_DIM - half), F32)], axis=1)
    s_hi = jnp.concatenate([jnp.zeros((s, half), F32), sin, jnp.zeros((s, pad), F32)], axis=1)
    k_tab = jnp.stack([c, s_lo, s_hi])
    return jnp.concatenate([k_tab * (LOG2_E * HEAD_DIM ** -0.5), k_tab], axis=0)


def _qkv_kernel(x_ref, g_ref, wq_ref, wk_ref, wv_ref, rope_ref, q_ref, k2_ref, vt_ref, kmean_ref, *, tm, gk):
    s = pl.program_id(1)
    half = ROT_DIM // 2
    blocks_per_tile = tm // MOBA_BLOCK
    hn = _rms(x_ref[0], g_ref[0, gk:gk + 1, :]).astype(BF16)
    for c in range(2 * MOBA_HEADS):
        if c % 2 == 0:
            w_ref = wq_ref if c < MOBA_HEADS else wk_ref
            c0 = c % MOBA_HEADS
            w2 = w_ref[0, :, c0 * HEAD_DIM:(c0 + 2) * HEAD_DIM].astype(BF16)
            y2 = jnp.dot(hn, w2, preferred_element_type=F32)
        y = y2[:, (c % 2) * HEAD_DIM:(c % 2 + 1) * HEAD_DIM]
        t = 0 if c < MOBA_HEADS else 3
        y = (y * rope_ref[t]
             + pltpu.roll(y, HEAD_DIM - half, 1) * rope_ref[t + 1]
             + pltpu.roll(y, half, 1) * rope_ref[t + 2])
        if c < MOBA_HEADS:
            q_ref[0, c] = y.astype(BF16)
        else:
            h = c - MOBA_HEADS
            for r in range(blocks_per_tile):
                row = jnp.mean(y[r * MOBA_BLOCK:(r + 1) * MOBA_BLOCK, :], axis=0, keepdims=True)
                kmean_ref[0, h, pl.ds(s * blocks_per_tile + r, 1), :] = row
            if h % 2 == 0:
                k_even = y
            else:
                k2_ref[0, h // 2] = jnp.concatenate([k_even, y], axis=1).astype(BF16)
    vt = lax.dot_general(wv_ref[0].astype(BF16), hn, (((0,), (1,)), ((), ())), preferred_element_type=F32)
    vt_ref[0] = vt.reshape(MOBA_HEADS, HEAD_DIM, tm).astype(BF16)


def _moba_qkv(x, gains, layer, gk, w_qkv, j, rope):
    b, s, d = x.shape
    tm = QKV_TILE
    nb = s // MOBA_BLOCK
    h = MOBA_HEADS
    assert s % tm == 0 and tm % MOBA_BLOCK == 0 and d == h * HEAD_DIM and w_qkv.shape[1:] == (d, 3 * d)
    tile = _nbytes((tm, d), F32)
    out_tile = _nbytes((3 * h, tm, HEAD_DIM), BF16)
    rope_tile = _nbytes((6, tm, HEAD_DIM), F32)
    weights = _nbytes((d, 3 * d), F32)
    w_part = lambda part: pl.BlockSpec((1, d, d), lambda bi, si: (j, 0, part))
    return pl.pallas_call(
        functools.partial(_qkv_kernel, tm=tm, gk=gk),
        grid=(b, s // tm),
        in_specs=[
            pl.BlockSpec((1, tm, d), lambda bi, si: (bi, si, 0)),
            _layer_block(gains, layer),
            w_part(0), w_part(1), w_part(2),
            pl.BlockSpec((6, tm, HEAD_DIM), lambda bi, si: (0, si, 0)),
        ],
        out_specs=[
            pl.BlockSpec((1, h, tm, HEAD_DIM), lambda bi, si: (bi, 0, si, 0)),
            pl.BlockSpec((1, h // 2, tm, 2 * HEAD_DIM), lambda bi, si: (bi, 0, si, 0)),
            pl.BlockSpec((1, h, HEAD_DIM, tm), lambda bi, si: (bi, 0, 0, si)),
            pl.BlockSpec((1, h, nb, HEAD_DIM), lambda bi, si: (bi, 0, 0, 0)),
        ],
        out_shape=[
            jax.ShapeDtypeStruct((b, h, s, HEAD_DIM), BF16),
            jax.ShapeDtypeStruct((b, h // 2, s, 2 * HEAD_DIM), BF16),
            jax.ShapeDtypeStruct((b, h, HEAD_DIM, s), BF16),
            jax.ShapeDtypeStruct((b, h, nb, HEAD_DIM), F32),
        ],
        compiler_params=_compiler_params(
            ("arbitrary", "arbitrary"), tile + out_tile + rope_tile + weights, 0, 4 * tile),
        name="moba_qkv",
    )(x, gains, w_qkv, w_qkv, w_qkv, rope)


def _attn_query_block(q_ref, k2_ref, vt_ref, km_ref, ot_ref, ii, qb):
    blk = MOBA_BLOCK
    hd = HEAD_DIM
    nb = k2_ref.shape[2] // blk
    nbp = GATE_ROWS
    gated = ii > MOBA_TOPK
    q_rows = slice(qb * blk, (qb + 1) * blk)

    key_id = lax.broadcasted_iota(jnp.int32, (blk, 2 * blk), 0)
    qry_id = lax.broadcasted_iota(jnp.int32, (blk, 2 * blk), 1) & (blk - 1)
    causal = key_id <= qry_id
    blk_id = lax.broadcasted_iota(jnp.int32, (nbp, blk), 0)
    zeros_q = jnp.zeros((blk, hd), BF16)
    ones = jnp.ones((SUM_ROWS, blk), BF16)

    def block_bias(q, h):
        km = km_ref[0, h]
        km = jnp.concatenate([km, jnp.zeros((nbp - nb, hd), F32)], axis=0)
        km_hi = km.astype(BF16)
        km_lo = (km - km_hi.astype(F32)).astype(BF16)
        gate = (lax.dot_general(km_hi, q, NT_DIMS, preferred_element_type=F32)
                + lax.dot_general(km_lo, q, NT_DIMS, preferred_element_type=F32))
        rank = jnp.zeros((nbp, blk), F32)
        for jp in range(ii):
            gj = gate[jp:jp + 1, :]
            beats = (gj > gate) | ((gj == gate) & (blk_id > jp))
            rank = rank + beats.astype(F32)
        return jnp.where(rank >= float(MOBA_TOPK), MASK_BIAS, 0.0)

    def pair(g, carry):
        q0 = q_ref[0, 2 * g, q_rows, :]
        q1 = q_ref[0, 2 * g + 1, q_rows, :]
        qd = jnp.concatenate([jnp.concatenate([q0, zeros_q], axis=1),
                              jnp.concatenate([zeros_q, q1], axis=1)], axis=0)
        if gated:
            bias = jnp.concatenate([block_bias(q0, 2 * g), block_bias(q1, 2 * g + 1)], axis=1)
        m = None
        acc = [None, None]
        order = [ii] + list(range(ii))

        def scores(j):
            sc = lax.dot_general(k2_ref[0, g, j * blk:(j + 1) * blk, :], qd, NT_DIMS,
                                 preferred_element_type=F32)
            if j == ii:
                sc = jnp.where(causal, sc, -jnp.inf)
            bm = jnp.max(sc, axis=0, keepdims=True)
            if gated and j < ii:
                bm = bm + bias[j:j + 1, :]
            return sc, bm

        def accumulate(j, p, alpha):
            for hh in range(2):
                cols = slice(hh * blk, (hh + 1) * blk)
                vt1 = jnp.concatenate([vt_ref[0, 2 * g + hh, :, j * blk:(j + 1) * blk], ones], axis=0)
                pv = jnp.dot(vt1, p[:, cols], preferred_element_type=F32)
                acc[hh] = pv if alpha is None else acc[hh] * alpha[:, cols] + pv

        n_blk = len(order)
        sc_q = [scores(j) for j in order[:SCORE_LOOKAHEAD]]
        pending = None
        for t, j in enumerate(order):
            sc, bm = sc_q.pop(0)
            if t + SCORE_LOOKAHEAD < n_blk:
                sc_q.append(scores(order[t + SCORE_LOOKAHEAD]))
            if pending is not None:
                accumulate(*pending)
            masked = gated and j < ii
            m_new = bm if m is None else jnp.maximum(m, bm)
            shift = bias[j:j + 1, :] - m_new if masked else -m_new
            p = jnp.exp2(sc + shift).astype(BF16)
            alpha = None if m is None else jnp.exp2(m - m_new)
            pending = (j, p, alpha)
            m = m_new
        accumulate(*pending)
        for hh in range(2):
            ot_ref[qb, 2 * g + hh] = (acc[hh][:hd, :] / acc[hh][hd:hd + 1, :]).astype(BF16)
        return carry

    lax.fori_loop(0, MOBA_HEADS // 2, pair, 0, unroll=True)


def _attn_kernel(q_ref, k2_ref, vt_ref, km_ref, x_ref, wo_ref, g_ref, o_ref, ot_ref, *, gk):
    i = pl.program_id(1)
    blk = MOBA_BLOCK
    nb = k2_ref.shape[2] // blk
    d = x_ref.shape[2]

    def grid_step(step):
        for qb in range(ATTN_QBLOCKS):
            _attn_query_block(q_ref, k2_ref, vt_ref, km_ref, ot_ref, step * ATTN_QBLOCKS + qb, qb)
        o_t = jnp.concatenate([ot_ref[qb].reshape(d, blk) for qb in range(ATTN_QBLOCKS)], axis=1)
        y = lax.dot_general(o_t, wo_ref[0].astype(BF16), TN_DIMS, preferred_element_type=F32)
        o_ref[0] = x_ref[0] + _rms(y, g_ref[0, gk:gk + 1, :])

    for step in range(nb // ATTN_QBLOCKS):
        pl.when(i == step)(functools.partial(grid_step, step))


def _moba_attn(x, q, k2, vt, kmean, w_o, j, gains, layer, gk):
    b, s, d = x.shape
    blk = MOBA_BLOCK
    nb = s // blk
    h = MOBA_HEADS
    rows = ATTN_QBLOCKS * blk
    assert nb <= GATE_ROWS and blk == 256 and h % 2 == 0 and nb % ATTN_QBLOCKS == 0
    q_tile = _nbytes((h, rows, HEAD_DIM), BF16)
    kv_tile = _nbytes((h, s, HEAD_DIM), BF16)
    x_tile = _nbytes((rows, d), F32)
    return pl.pallas_call(
        functools.partial(_attn_kernel, gk=gk),
        grid=(b, nb // ATTN_QBLOCKS),
        in_specs=[
            pl.BlockSpec((1, h, rows, HEAD_DIM), lambda bi, qi: (bi, 0, qi, 0)),
            pl.BlockSpec((1, h // 2, s, 2 * HEAD_DIM), lambda bi, qi: (bi, 0, 0, 0)),
            pl.BlockSpec((1, h, HEAD_DIM, s), lambda bi, qi: (bi, 0, 0, 0)),
            pl.BlockSpec((1, h, nb, HEAD_DIM), lambda bi, qi: (bi, 0, 0, 0)),
            pl.BlockSpec((1, rows, d), lambda bi, qi: (bi, qi, 0)),
            _layer_block(w_o, j),
            _layer_block(gains, layer),
        ],
        out_specs=pl.BlockSpec((1, rows, d), lambda bi, qi: (bi, qi, 0)),
        out_shape=jax.ShapeDtypeStruct(x.shape, F32),
        scratch_shapes=[pltpu.VMEM((ATTN_QBLOCKS, h, HEAD_DIM, blk), BF16)],
        compiler_params=_compiler_params(
            ("arbitrary", "arbitrary"), q_tile + 2 * kv_tile + 2 * x_tile + _nbytes((d, d), F32),
            x_tile, 8 * _nbytes((blk, s), F32)),
        name="moba_attention",
    )(q, k2, vt, kmean, x, w_o, gains)


def _memkv_kernel(mem_ref, g_ref, w_ref, o_ref):
    nb, m, d = mem_ref.shape
    mn = _rms(mem_ref[...].reshape(nb * m, d), g_ref[0]).astype(BF16)
    kv = jnp.dot(mn, w_ref[0].astype(BF16), preferred_element_type=F32).astype(BF16)
    o_ref[0] = kv.reshape(nb, m, kv.shape[1])


def _mem_kv(mem, mem_norm, w_kv):
    depth, d, d2 = w_kv.shape
    b, m, _ = mem.shape
    nb = MEMKV_BATCH
    assert b % nb == 0 and m % 8 == 0
    return pl.pallas_call(
        _memkv_kernel,
        grid=(depth, b // nb),
        in_specs=[
            pl.BlockSpec((nb, m, d), lambda li, bi: (bi, 0, 0)),
            pl.BlockSpec((1, 1, d), lambda li, bi: (li, 0, 0)),
            pl.BlockSpec((1, d, d2), lambda li, bi: (li, 0, 0)),
        ],
        out_specs=pl.BlockSpec((1, nb, m, d2), lambda li, bi: (li, bi, 0, 0)),
        out_shape=jax.ShapeDtypeStruct((depth, b, m, d2), BF16),
        compiler_params=_compiler_params(
            ("arbitrary", "arbitrary"),
            _nbytes((nb * m, d), F32) + _nbytes((d, d2), F32) + _nbytes((nb * m, d2), BF16), 0,
            3 * _nbytes((nb * m, d2), F32)),
        name="memory_kv",
    )(mem, mem_norm.reshape(depth, 1, d), w_kv)


def _xattn_kernel(x_ref, g_ref, wq_ref, kv_ref, wo_ref, o_ref, *, gk):
    d = x_ref.shape[2]
    hd = MEM_HEAD_DIM
    subs = _sub_tiles(x_ref.shape[1], XATTN_SUB_ROWS)
    heads = range(MEM_HEADS)
    xs = [x_ref[0, rows, :] for rows in subs]
    hn = [_rms(x, g_ref[0, gk:gk + 1, :]).astype(BF16) for x in xs]
    wq = wq_ref[0].astype(BF16)
    q_scale = LOG2_E * hd ** -0.5
    q = [(jnp.dot(h, wq, preferred_element_type=F32) * q_scale).astype(BF16) for h in hn]
    sc = [[lax.dot_general(qs[:, h * hd:(h + 1) * hd], kv_ref[0, 0, :, h * hd:(h + 1) * hd], NT_DIMS,
                           preferred_element_type=F32) for h in heads] for qs in q]
    p, inv_l = [], []
    for sc_s in sc:
        e = [jnp.exp2(t - jnp.max(t, axis=1, keepdims=True)) for t in sc_s]
        inv_l.append([1.0 / jnp.sum(t, axis=1, keepdims=True) for t in e])
        p.append([t.astype(BF16) for t in e])
    wo = wo_ref[0].astype(BF16)
    for rows, x, p_s, inv_s in zip(subs, xs, p, inv_l):
        outs = [jnp.dot(p_s[h], kv_ref[0, 0, :, d + h * hd:d + (h + 1) * hd], preferred_element_type=F32) * inv_s[h]
                for h in heads]
        o = jnp.concatenate(outs, axis=1).astype(BF16)
        y = jnp.dot(o, wo, preferred_element_type=F32)
        o_ref[0, rows, :] = x + _rms(y, g_ref[0, gk + 1:gk + 2, :])


def _xattn_layer(x, gains, layer, gk, w_q, kv_all, w_o):
    b, s, d = x.shape
    tm = XATTN_TILE
    m = kv_all.shape[2]
    assert s % tm == 0 and d == MEM_HEADS * MEM_HEAD_DIM
    tile = _nbytes((tm, d), F32)
    return pl.pallas_call(
        functools.partial(_xattn_kernel, gk=gk),
        grid=(b, s // tm),
        in_specs=[
            pl.BlockSpec((1, tm, d), lambda bi, si: (bi, si, 0)),
            _layer_block(gains, layer),
            _layer_block(w_q, layer),
            pl.BlockSpec((1, 1, m, 2 * d), lambda bi, si: (layer, bi, 0, 0)),
            _layer_block(w_o, layer),
        ],
        out_specs=pl.BlockSpec((1, tm, d), lambda bi, si: (bi, si, 0)),
        out_shape=jax.ShapeDtypeStruct(x.shape, F32),
        compiler_params=_compiler_params(
            ("arbitrary", "arbitrary"),
            2 * tile + 2 * _nbytes((d, d), F32) + _nbytes((m, 2 * d), BF16), 0, 6 * tile),
        name="memory_xattn_sublayer",
    )(x, gains, w_q, kv_all, w_o)


def _mlp_kernel(x_ref, g_ref, w1_hbm, w2_hbm, o_ref, w1_ref, w2_ref, sem, *, gk, layer):
    d_ff = w1_ref.shape[1]
    n_chunks = d_ff // FF_CHUNK
    step = pl.program_id(0)

    def chunk_copies(c):
        ff = pl.ds(c * FF_CHUNK, FF_CHUNK)
        return (pltpu.make_async_copy(w1_hbm.at[layer, :, ff], w1_ref.at[:, ff], sem.at[0, c]),
                pltpu.make_async_copy(w2_hbm.at[layer, ff, :], w2_ref.at[ff, :], sem.at[1, c]))

    @pl.when(step == 0)
    def _():
        for c in range(n_chunks):
            for copy in chunk_copies(c):
                copy.start()

    def body(wait_for_weights):
        subs = _sub_tiles(x_ref.shape[0], MLP_SUB_ROWS)
        xs = [x_ref[rows, :] for rows in subs]
        hn = [_rms(x, g_ref[0, gk:gk + 1, :]).astype(BF16) for x in xs]
        acc = [jnp.zeros(x.shape, F32) for x in xs]
        for c in range(n_chunks):
            if wait_for_weights:
                for copy in chunk_copies(c):
                    copy.wait()
            cols = slice(c * FF_CHUNK, (c + 1) * FF_CHUNK)
            w1c = w1_ref[:, cols].astype(BF16)
            w2c = w2_ref[cols, :].astype(BF16)
            a = [jnp.maximum(jnp.dot(h, w1c, preferred_element_type=F32), 0.0) for h in hn]
            acc = [t + jnp.dot((u * u).astype(BF16), w2c, preferred_element_type=F32) for t, u in zip(acc, a)]
        for rows, x, t in zip(subs, xs, acc):
            o_ref[rows, :] = x + _rms(t, g_ref[0, gk + 1:gk + 2, :])

    pl.when(step == 0)(functools.partial(body, True))
    pl.when(step != 0)(functools.partial(body, False))


def _mlp_layer(x, gains, layer, gk, w1, w2):
    b, s, d = x.shape
    tm = MLP_TILE
    rows = b * s
    d_ff = w1.shape[2]
    assert rows % tm == 0 and d_ff % FF_CHUNK == 0
    tile = _nbytes((tm, d), F32)
    weights = 2 * _nbytes((d, d_ff), F32)
    out = pl.pallas_call(
        functools.partial(_mlp_kernel, gk=gk, layer=layer),
        grid=(rows // tm,),
        in_specs=[
            pl.BlockSpec((tm, d), lambda ri: (ri, 0)),
            _layer_block(gains, layer),
            pl.BlockSpec(memory_space=pl.ANY),
            pl.BlockSpec(memory_space=pl.ANY),
        ],
        out_specs=pl.BlockSpec((tm, d), lambda ri: (ri, 0)),
        out_shape=jax.ShapeDtypeStruct((rows, d), F32),
        scratch_shapes=[
            pltpu.VMEM((d, d_ff), F32),
            pltpu.VMEM((d_ff, d), F32),
            pltpu.SemaphoreType.DMA((2, d_ff // FF_CHUNK)),
        ],
        compiler_params=_compiler_params(("arbitrary",), 2 * tile, weights, 6 * tile),
        name="mlp_sublayer",
    )(x.reshape(rows, d), gains, w1, w2)
    return out.reshape(b, s, d)


def kernel(x, mem, norm_gains, mem_norm, pool_w_in, pool_w_group, pool_scale, moba_w_qkv, moba_w_o,
           xa_w_q, xa_w_kv, xa_w_o, mlp_w1, mlp_w2):
    depth = norm_gains.shape[0]
    s, d = x.shape[1], x.shape[2]
    assert norm_gains.shape[1] == N_NORMS
    rope = _rope_tables(s)
    scale = pool_scale.reshape(pool_scale.shape[0], 1, d)
    kv_all = _mem_kv(mem, mem_norm, xa_w_kv)
    for i in range(depth):
        j = i // 2
        if i % 2 == 0:
            x = _pool_layer(x, norm_gains, i, 0, pool_w_in, pool_w_group, scale, j)
        else:
            q, k2, vt, kmean = _moba_qkv(x, norm_gains, i, 0, moba_w_qkv, j, rope)
            x = _moba_attn(x, q, k2, vt, kmean, moba_w_o, j, norm_gains, i, 1)
        x = _xattn_layer(x, norm_gains, i, 2, xa_w_q, kv_all, xa_w_o)
        x = _mlp_layer(x, norm_gains, i, 4, mlp_w1, mlp_w2)
    return x
```

```python
import functools

import jax
import jax.numpy as jnp
from jax import lax
from jax.experimental import pallas as pl
from jax.experimental.pallas import tpu as pltpu

F32 = jnp.float32
BF16 = jnp.bfloat16

RMS_EPS = 1e-6
N_NORMS = 6
POOL_WINDOWS = (2, 4, 8, 16)
POOL_GW = 256
POOL_HALO = 16
MOBA_HEADS = 8
HEAD_DIM = 128
ROT_DIM = 32
ROPE_THETA = 500000.0
MOBA_BLOCK = 256
MOBA_TOPK = 3
MEM_HEADS = 4
MEM_HEAD_DIM = 256

MLP_TILE = 512
POOL_TILE = 512
QKV_TILE = 1024
XATTN_TILE = 1024
ATTN_QBLOCKS = 2
SCORE_LOOKAHEAD = 2
MEMKV_BATCH = 2
XATTN_SUB_ROWS = 256
MLP_SUB_ROWS = 256
FF_CHUNK = 1024
V7X_VMEM_BYTES = 64 * 1024 * 1024
V7X_VMEM_USABLE = 56 * 1024 * 1024

GATE_ROWS = 16
MASK_BIAS = -1e30
SUM_ROWS = 16
LOG2_E = 1.4426950408889634

NT_DIMS = (((1,), (1,)), ((), ()))
TN_DIMS = (((0,), (0,)), ((), ()))


def _nbytes(shape, dtype):
    n = 1
    for d in shape:
        n *= d
    return n * jnp.dtype(dtype).itemsize


def _compiler_params(semantics, pipelined_bytes, resident_bytes, temp_bytes):
    need = 2 * pipelined_bytes + resident_bytes + temp_bytes
    return pltpu.CompilerParams(
        dimension_semantics=semantics,
        vmem_limit_bytes=int(min(max(need, 16 * 1024 * 1024), V7X_VMEM_USABLE)),
    )


def _layer_block(stack, layer, **kwargs):
    shape = (1,) + tuple(stack.shape[1:])
    zeros = (0,) * (stack.ndim - 1)
    return pl.BlockSpec(shape, lambda *_: (layer,) + zeros, **kwargs)


def _sub_tiles(rows, sub_rows):
    assert rows % sub_rows == 0
    return [slice(r, r + sub_rows) for r in range(0, rows, sub_rows)]


def _rms(xf, gain):
    ms = jnp.mean(xf * xf, axis=-1, keepdims=True)
    return xf * lax.rsqrt(ms + RMS_EPS) * gain


def _pool_kernel(x_ref, g_ref, win_ref, wg_ref, scale_ref, o_ref, carry_ref, *, tm, gk):
    s = pl.program_id(1)

    @pl.when(s == 0)
    def _():
        carry_ref[...] = jnp.zeros_like(carry_ref)

    x = x_ref[0]
    hn = _rms(x, g_ref[0, gk:gk + 1, :]).astype(BF16)
    u = jnp.dot(hn, win_ref[0].astype(BF16), preferred_element_type=F32)
    ext = jnp.concatenate([carry_ref[...], u], axis=0)
    carry_ref[...] = u[tm - POOL_HALO:, :]

    pos = s * tm + lax.broadcasted_iota(jnp.int32, (tm, 1), 0)
    ys = []
    for g, w in enumerate(POOL_WINDOWS):
        cols = slice(g * POOL_GW, (g + 1) * POOL_GW)
        t = ext[:, cols]
        shift = 1
        while shift < w:
            t = t + pltpu.roll(t, shift, 0)
            shift *= 2
        win_sum = t[POOL_HALO:, :]
        cnt = jnp.minimum(pos + 1, w).astype(F32)
        pooled = win_sum / cnt - u[:, cols]
        ys.append(jnp.dot(pooled.astype(BF16), wg_ref[0, g].astype(BF16), preferred_element_type=F32))
    y = jnp.concatenate(ys, axis=1) * scale_ref[0]
    o_ref[0] = x + _rms(y, g_ref[0, gk + 1:gk + 2, :])


def _pool_layer(x, gains, layer, gk, w_in, w_group, scale, j):
    b, s, d = x.shape
    tm = POOL_TILE
    assert s % tm == 0 and tm % 8 == 0 and tm >= POOL_HALO
    tile = _nbytes((tm, d), F32)
    weights = _nbytes(w_in.shape[1:], F32) + _nbytes(w_group.shape[1:], F32)
    return pl.pallas_call(
        functools.partial(_pool_kernel, tm=tm, gk=gk),
        grid=(b, s // tm),
        in_specs=[
            pl.BlockSpec((1, tm, d), lambda bi, si: (bi, si, 0)),
            _layer_block(gains, layer),
            _layer_block(w_in, j),
            _layer_block(w_group, j),
            _layer_block(scale, j),
        ],
        out_specs=pl.BlockSpec((1, tm, d), lambda bi, si: (bi, si, 0)),
        out_shape=jax.ShapeDtypeStruct(x.shape, F32),
        scratch_shapes=[pltpu.VMEM((POOL_HALO, d), F32)],
        compiler_params=_compiler_params(("arbitrary", "arbitrary"), 2 * tile + weights, 0, 8 * tile),
        name="pool_sublayer",
    )(x, gains, w_in, w_group, scale)


def _rope_tables(s):
    half = ROT_DIM // 2
    pos = jnp.arange(s, dtype=F32)
    inv_freq = ROPE_THETA ** (-jnp.arange(0, ROT_DIM, 2, dtype=F32) / ROT_DIM)
    ang = pos[:, None] * inv_freq[None, :]
    cos, sin = jnp.cos(ang), jnp.sin(ang)
    pad = HEAD_DIM - ROT_DIM
    c = jnp.concatenate([cos, cos, jnp.ones((s, pad), F32)], axis=1)
    s_lo = jnp.concatenate([-sin, jnp.zeros((s, HEAD_DIM - half), F32)], axis=1)
    s_hi = jnp.concatenate([jnp.zeros((s, half), F32), sin, jnp.zeros((s, pad), F32)], axis=1)
    k_tab = jnp.stack([c, s_lo, s_hi])
    return jnp.concatenate([k_tab * (LOG2_E * HEAD_DIM ** -0.5), k_tab], axis=0)


def _qkv_kernel(x_ref, g_ref, wq_ref, wk_ref, wv_ref, rope_ref, q_ref, k2_ref, vt_ref, *, tm, gk):
    half = ROT_DIM // 2
    hn = _rms(x_ref[0], g_ref[0, gk:gk + 1, :]).astype(BF16)
    for c in range(2 * MOBA_HEADS):
        if c % 2 == 0:
            w_ref = wq_ref if c < MOBA_HEADS else wk_ref
            c0 = c % MOBA_HEADS
            w2 = w_ref[0, :, c0 * HEAD_DIM:(c0 + 2) * HEAD_DIM].astype(BF16)
            y2 = jnp.dot(hn, w2, preferred_element_type=F32)
        y = y2[:, (c % 2) * HEAD_DIM:(c % 2 + 1) * HEAD_DIM]
        t = 0 if c < MOBA_HEADS else 3
        y = (y * rope_ref[t]
             + pltpu.roll(y, HEAD_DIM - half, 1) * rope_ref[t + 1]
             + pltpu.roll(y, half, 1) * rope_ref[t + 2])
        if c < MOBA_HEADS:
            q_ref[0, c] = y.astype(BF16)
        else:
            h = c - MOBA_HEADS
            if h % 2 == 0:
                k_even = y
            else:
                k2_ref[0, h // 2] = jnp.concatenate([k_even, y], axis=1).astype(BF16)
    vt = lax.dot_general(wv_ref[0].astype(BF16), hn, (((0,), (1,)), ((), ())), preferred_element_type=F32)
    vt_ref[0] = vt.reshape(MOBA_HEADS, HEAD_DIM, tm).astype(BF16)


def _moba_qkv(x, gains, layer, gk, w_qkv, j, rope):
    b, s, d = x.shape
    tm = QKV_TILE
    h = MOBA_HEADS
    assert s % tm == 0 and tm % MOBA_BLOCK == 0 and d == h * HEAD_DIM and w_qkv.shape[1:] == (d, 3 * d)
    tile = _nbytes((tm, d), F32)
    out_tile = _nbytes((3 * h, tm, HEAD_DIM), BF16)
    rope_tile = _nbytes((6, tm, HEAD_DIM), F32)
    weights = _nbytes((d, 3 * d), F32)
    w_part = lambda part: pl.BlockSpec((1, d, d), lambda bi, si: (j, 0, part), pipeline_mode=pl.Buffered(1))
    return pl.pallas_call(
        functools.partial(_qkv_kernel, tm=tm, gk=gk),
        grid=(b, s // tm),
        in_specs=[
            pl.BlockSpec((1, tm, d), lambda bi, si: (bi, si, 0)),
            _layer_block(gains, layer),
            w_part(0), w_part(1), w_part(2),
            pl.BlockSpec((6, tm, HEAD_DIM), lambda bi, si: (0, si, 0)),
        ],
        out_specs=[
            pl.BlockSpec((1, h, tm, HEAD_DIM), lambda bi, si: (bi, 0, si, 0)),
            pl.BlockSpec((1, h // 2, tm, 2 * HEAD_DIM), lambda bi, si: (bi, 0, si, 0)),
            pl.BlockSpec((1, h, HEAD_DIM, tm), lambda bi, si: (bi, 0, 0, si)),
        ],
        out_shape=[
            jax.ShapeDtypeStruct((b, h, s, HEAD_DIM), BF16),
            jax.ShapeDtypeStruct((b, h // 2, s, 2 * HEAD_DIM), BF16),
            jax.ShapeDtypeStruct((b, h, HEAD_DIM, s), BF16),
        ],
        compiler_params=_compiler_params(
            ("arbitrary", "arbitrary"), tile + out_tile + rope_tile, weights, 4 * tile),
        name="moba_qkv",
    )(x, gains, w_qkv, w_qkv, w_qkv, rope)


def _attn_query_block(q_ref, k2_ref, vt_ref, km_ref, ot_ref, ii, qb):
    blk = MOBA_BLOCK
    hd = HEAD_DIM
    nbp = GATE_ROWS
    gated = ii > MOBA_TOPK
    q_rows = slice(qb * blk, (qb + 1) * blk)

    key_id = lax.broadcasted_iota(jnp.int32, (blk, 2 * blk), 0)
    qry_id = lax.broadcasted_iota(jnp.int32, (blk, 2 * blk), 1) & (blk - 1)
    causal = key_id <= qry_id
    blk_id = lax.broadcasted_iota(jnp.int32, (nbp, blk), 0)
    zeros_q = jnp.zeros((blk, hd), BF16)
    ones = jnp.ones((SUM_ROWS, blk), BF16)

    def block_bias(q, h):
        km = km_ref[h]
        km_hi = km.astype(BF16)
        km_lo = (km - km_hi.astype(F32)).astype(BF16)
        gate = (lax.dot_general(km_hi, q, NT_DIMS, preferred_element_type=F32)
                + lax.dot_general(km_lo, q, NT_DIMS, preferred_element_type=F32))
        rank = jnp.zeros((nbp, blk), F32)
        for jp in range(ii):
            gj = gate[jp:jp + 1, :]
            beats = (gj > gate) | ((gj == gate) & (blk_id > jp))
            rank = rank + beats.astype(F32)
        return jnp.where(rank >= float(MOBA_TOPK), MASK_BIAS, 0.0)

    def pair(g, carry):
        q0 = q_ref[0, 2 * g, q_rows, :]
        q1 = q_ref[0, 2 * g + 1, q_rows, :]
        qd = jnp.concatenate([jnp.concatenate([q0, zeros_q], axis=1),
                              jnp.concatenate([zeros_q, q1], axis=1)], axis=0)
        if gated:
            bias = jnp.concatenate([block_bias(q0, 2 * g), block_bias(q1, 2 * g + 1)], axis=1)
        m = None
        acc = [None, None]
        order = [ii] + list(range(ii))

        def scores(j):
            sc = lax.dot_general(k2_ref[0, g, j * blk:(j + 1) * blk, :], qd, NT_DIMS,
                                 preferred_element_type=F32)
            if j == ii:
                sc = jnp.where(causal, sc, -jnp.inf)
            bm = jnp.max(sc, axis=0, keepdims=True)
            if gated and j < ii:
                bm = bm + bias[j:j + 1, :]
            return sc, bm

        def accumulate(j, p, alpha):
            for hh in range(2):
                cols = slice(hh * blk, (hh + 1) * blk)
                vt1 = jnp.concatenate([vt_ref[0, 2 * g + hh, :, j * blk:(j + 1) * blk], ones], axis=0)
                pv = jnp.dot(vt1, p[:, cols], preferred_element_type=F32)
                acc[hh] = pv if alpha is None else acc[hh] * alpha[:, cols] + pv

        n_blk = len(order)
        sc_q = [scores(j) for j in order[:SCORE_LOOKAHEAD]]
        pending = None
        for t, j in enumerate(order):
            sc, bm = sc_q.pop(0)
            if t + SCORE_LOOKAHEAD < n_blk:
                sc_q.append(scores(order[t + SCORE_LOOKAHEAD]))
            if pending is not None:
                accumulate(*pending)
            masked = gated and j < ii
            m_new = bm if m is None else jnp.maximum(m, bm)
            shift = bias[j:j + 1, :] - m_new if masked else -m_new
            p = jnp.exp2(sc + shift).astype(BF16)
            alpha = None if m is None else jnp.exp2(m - m_new)
            pending = (j, p, alpha)
            m = m_new
        accumulate(*pending)
        for hh in range(2):
            ot_ref[qb, 2 * g + hh] = (acc[hh][:hd, :] / acc[hh][hd:hd + 1, :]).astype(BF16)
        return carry

    lax.fori_loop(0, MOBA_HEADS // 2, pair, 0, unroll=True)


def _attn_kernel(q_ref, k2_ref, vt_ref, x_ref, wo_ref, g_ref, o_ref, ot_ref, km_ref, *, gk):
    i = pl.program_id(1)
    blk = MOBA_BLOCK
    hd = HEAD_DIM
    nb = k2_ref.shape[2] // blk
    d = x_ref.shape[2]

    @pl.when(i == 0)
    def _():
        km_ref[...] = jnp.zeros_like(km_ref)
        for g in range(MOBA_HEADS // 2):
            for j in range(nb):
                mean = jnp.mean(k2_ref[0, g, j * blk:(j + 1) * blk, :].astype(F32), axis=0, keepdims=True)
                km_ref[2 * g, j:j + 1, :] = mean[:, :hd]
                km_ref[2 * g + 1, j:j + 1, :] = mean[:, hd:]

    def grid_step(step):
        for qb in range(ATTN_QBLOCKS):
            _attn_query_block(q_ref, k2_ref, vt_ref, km_ref, ot_ref, step * ATTN_QBLOCKS + qb, qb)
        o_t = jnp.concatenate([ot_ref[qb].reshape(d, blk) for qb in range(ATTN_QBLOCKS)], axis=1)
        y = lax.dot_general(o_t, wo_ref[0].astype(BF16), TN_DIMS, preferred_element_type=F32)
        o_ref[0] = x_ref[0] + _rms(y, g_ref[0, gk:gk + 1, :])

    for step in range(nb // ATTN_QBLOCKS):
        pl.when(i == step)(functools.partial(grid_step, step))


def _moba_attn(x, q, k2, vt, w_o, j, gains, layer, gk):
    b, s, d = x.shape
    blk = MOBA_BLOCK
    nb = s // blk
    h = MOBA_HEADS
    rows = ATTN_QBLOCKS * blk
    assert nb <= GATE_ROWS and blk == 256 and h % 2 == 0 and nb % ATTN_QBLOCKS == 0
    q_tile = _nbytes((h, rows, HEAD_DIM), BF16)
    kv_tile = _nbytes((h, s, HEAD_DIM), BF16)
    x_tile = _nbytes((rows, d), F32)
    return pl.pallas_call(
        functools.partial(_attn_kernel, gk=gk),
        grid=(b, nb // ATTN_QBLOCKS),
        in_specs=[
            pl.BlockSpec((1, h, rows, HEAD_DIM), lambda bi, qi: (bi, 0, qi, 0)),
            pl.BlockSpec((1, h // 2, s, 2 * HEAD_DIM), lambda bi, qi: (bi, 0, 0, 0)),
            pl.BlockSpec((1, h, HEAD_DIM, s), lambda bi, qi: (bi, 0, 0, 0)),
            pl.BlockSpec((1, rows, d), lambda bi, qi: (bi, qi, 0)),
            _layer_block(w_o, j),
            _layer_block(gains, layer),
        ],
        out_specs=pl.BlockSpec((1, rows, d), lambda bi, qi: (bi, qi, 0)),
        out_shape=jax.ShapeDtypeStruct(x.shape, F32),
        scratch_shapes=[pltpu.VMEM((ATTN_QBLOCKS, h, HEAD_DIM, blk), BF16),
                        pltpu.VMEM((h, GATE_ROWS, HEAD_DIM), F32)],
        compiler_params=_compiler_params(
            ("arbitrary", "arbitrary"), q_tile + 2 * kv_tile + 2 * x_tile + _nbytes((d, d), F32),
            x_tile, 8 * _nbytes((blk, s), F32)),
        name="moba_attention",
    )(q, k2, vt, x, w_o, gains)


def _memkv_kernel(mem_ref, g_ref, w_ref, o_ref):
    nb, m, d = mem_ref.shape
    mn = _rms(mem_ref[...].reshape(nb * m, d), g_ref[0]).astype(BF16)
    kv = jnp.dot(mn, w_ref[0].astype(BF16), preferred_element_type=F32).astype(BF16)
    o_ref[0] = kv.reshape(nb, m, kv.shape[1])


def _mem_kv(mem, mem_norm, w_kv):
    depth, d, d2 = w_kv.shape
    b, m, _ = mem.shape
    nb = MEMKV_BATCH
    assert b % nb == 0 and m % 8 == 0
    return pl.pallas_call(
        _memkv_kernel,
        grid=(depth, b // nb),
        in_specs=[
            pl.BlockSpec((nb, m, d), lambda li, bi: (bi, 0, 0)),
            pl.BlockSpec((1, 1, d), lambda li, bi: (li, 0, 0)),
            pl.BlockSpec((1, d, d2), lambda li, bi: (li, 0, 0)),
        ],
        out_specs=pl.BlockSpec((1, nb, m, d2), lambda li, bi: (li, bi, 0, 0)),
        out_shape=jax.ShapeDtypeStruct((depth, b, m, d2), BF16),
        compiler_params=_compiler_params(
            ("arbitrary", "arbitrary"),
            _nbytes((nb * m, d), F32) + _nbytes((d, d2), F32) + _nbytes((nb * m, d2), BF16), 0,
            3 * _nbytes((nb * m, d2), F32)),
        name="memory_kv",
    )(mem, mem_norm.reshape(depth, 1, d), w_kv)


def _xattn_kernel(x_ref, g_ref, wq_ref, kv_ref, wo_ref, o_ref, *, gk):
    d = x_ref.shape[2]
    hd = MEM_HEAD_DIM
    subs = _sub_tiles(x_ref.shape[1], XATTN_SUB_ROWS)
    heads = range(MEM_HEADS)
    xs = [x_ref[0, rows, :] for rows in subs]
    hn = [_rms(x, g_ref[0, gk:gk + 1, :]).astype(BF16) for x in xs]
    wq = wq_ref[0].astype(BF16)
    q_scale = LOG2_E * hd ** -0.5
    q = [(jnp.dot(h, wq, preferred_element_type=F32) * q_scale).astype(BF16) for h in hn]
    sc = [[lax.dot_general(qs[:, h * hd:(h + 1) * hd], kv_ref[0, 0, :, h * hd:(h + 1) * hd], NT_DIMS,
                           preferred_element_type=F32) for h in heads] for qs in q]
    p, inv_l = [], []
    for sc_s in sc:
        e = [jnp.exp2(t - jnp.max(t, axis=1, keepdims=True)) for t in sc_s]
        inv_l.append([1.0 / jnp.sum(t, axis=1, keepdims=True) for t in e])
        p.append([t.astype(BF16) for t in e])
    wo = wo_ref[0].astype(BF16)
    for rows, x, p_s, inv_s in zip(subs, xs, p, inv_l):
        outs = [jnp.dot(p_s[h], kv_ref[0, 0, :, d + h * hd:d + (h + 1) * hd], preferred_element_type=F32) * inv_s[h]
                for h in heads]
        o = jnp.concatenate(outs, axis=1).astype(BF16)
        y = jnp.dot(o, wo, preferred_element_type=F32)
        o_ref[0, rows, :] = x + _rms(y, g_ref[0, gk + 1:gk + 2, :])


def _xattn_layer(x, gains, layer, gk, w_q, kv_all, w_o):
    b, s, d = x.shape
    tm = XATTN_TILE
    m = kv_all.shape[2]
    assert s % tm == 0 and d == MEM_HEADS * MEM_HEAD_DIM
    tile = _nbytes((tm, d), F32)
    return pl.pallas_call(
        functools.partial(_xattn_kernel, gk=gk),
        grid=(b, s // tm),
        in_specs=[
            pl.BlockSpec((1, tm, d), lambda bi, si: (bi, si, 0)),
            _layer_block(gains, layer),
            _layer_block(w_q, layer),
            pl.BlockSpec((1, 1, m, 2 * d), lambda bi, si: (layer, bi, 0, 0)),
            _layer_block(w_o, layer),
        ],
        out_specs=pl.BlockSpec((1, tm, d), lambda bi, si: (bi, si, 0)),
        out_shape=jax.ShapeDtypeStruct(x.shape, F32),
        compiler_params=_compiler_params(
            ("arbitrary", "arbitrary"),
            2 * tile + 2 * _nbytes((d, d), F32) + _nbytes((m, 2 * d), BF16), 0, 6 * tile),
        name="memory_xattn_sublayer",
    )(x, gains, w_q, kv_all, w_o)


def _mlp_kernel(x_ref, g_ref, w1_hbm, w2_hbm, o_ref, w1_ref, w2_ref, sem, *, gk, layer):
    d_ff = w1_ref.shape[1]
    n_chunks = d_ff // FF_CHUNK
    step = pl.program_id(0)

    def chunk_copies(c):
        ff = pl.ds(c * FF_CHUNK, FF_CHUNK)
        return (pltpu.make_async_copy(w1_hbm.at[layer, :, ff], w1_ref.at[:, ff], sem.at[0, c]),
                pltpu.make_async_copy(w2_hbm.at[layer, ff, :], w2_ref.at[ff, :], sem.at[1, c]))

    @pl.when(step == 0)
    def _():
        for c in range(n_chunks):
            for copy in chunk_copies(c):
                copy.start()

    def body(wait_for_weights):
        subs = _sub_tiles(x_ref.shape[0], MLP_SUB_ROWS)
        xs = [x_ref[rows, :] for rows in subs]
        hn = [_rms(x, g_ref[0, gk:gk + 1, :]).astype(BF16) for x in xs]
        acc = [jnp.zeros(x.shape, F32) for x in xs]
        for c in range(n_chunks):
            if wait_for_weights:
                for copy in chunk_copies(c):
                    copy.wait()
            cols = slice(c * FF_CHUNK, (c + 1) * FF_CHUNK)
            w1c = w1_ref[:, cols].astype(BF16)
            w2c = w2_ref[cols, :].astype(BF16)
            a = [jnp.maximum(jnp.dot(h, w1c, preferred_element_type=F32), 0.0) for h in hn]
            acc = [t + jnp.dot((u * u).astype(BF16), w2c, preferred_element_type=F32) for t, u in zip(acc, a)]
        for rows, x, t in zip(subs, xs, acc):
            o_ref[rows, :] = x + _rms(t, g_ref[0, gk + 1:gk + 2, :])

    pl.when(step == 0)(functools.partial(body, True))
    pl.when(step != 0)(functools.partial(body, False))


def _mlp_layer(x, gains, layer, gk, w1, w2):
    b, s, d = x.shape
    tm = MLP_TILE
    rows = b * s
    d_ff = w1.shape[2]
    assert rows % tm == 0 and d_ff % FF_CHUNK == 0
    tile = _nbytes((tm, d), F32)
    weights = 2 * _nbytes((d, d_ff), F32)
    out = pl.pallas_call(
        functools.partial(_mlp_kernel, gk=gk, layer=layer),
        grid=(rows // tm,),
        in_specs=[
            pl.BlockSpec((tm, d), lambda ri: (ri, 0)),
            _layer_block(gains, layer),
            pl.BlockSpec(memory_space=pl.ANY),
            pl.BlockSpec(memory_space=pl.ANY),
        ],
        out_specs=pl.BlockSpec((tm, d), lambda ri: (ri, 0)),
        out_shape=jax.ShapeDtypeStruct((rows, d), F32),
        scratch_shapes=[
            pltpu.VMEM((d, d_ff), F32),
            pltpu.VMEM((d_ff, d), F32),
            pltpu.SemaphoreType.DMA((2, d_ff // FF_CHUNK)),
        ],
        compiler_params=_compiler_params(("arbitrary",), 2 * tile, weights, 6 * tile),
        name="mlp_sublayer",
    )(x.reshape(rows, d), gains, w1, w2)
    return out.reshape(b, s, d)


def kernel(x, mem, norm_gains, mem_norm, pool_w_in, pool_w_group, pool_scale, moba_w_qkv, moba_w_o,
           xa_w_q, xa_w_kv, xa_w_o, mlp_w1, mlp_w2):
    depth = norm_gains.shape[0]
    s, d = x.shape[1], x.shape[2]
    assert norm_gains.shape[1] == N_NORMS
    rope = _rope_tables(s)
    scale = pool_scale.reshape(pool_scale.shape[0], 1, d)
    kv_all = _mem_kv(mem, mem_norm, xa_w_kv)
    for i in range(depth):
        j = i // 2
        if i % 2 == 0:
            x = _pool_layer(x, norm_gains, i, 0, pool_w_in, pool_w_group, scale, j)
        else:
            q, k2, vt = _moba_qkv(x, norm_gains, i, 0, moba_w_qkv, j, rope)
            x = _moba_attn(x, q, k2, vt, moba_w_o, j, norm_gains, i, 1)
        x = _xattn_layer(x, norm_gains, i, 2, xa_w_q, kv_all, xa_w_o)
        x = _mlp_layer(x, norm_gains, i, 4, mlp_w1, mlp_w2)
    return x
```

```python
import functools

import jax
import jax.numpy as jnp
from jax import lax
from jax.experimental import pallas as pl
from jax.experimental.pallas import tpu as pltpu

F32 = jnp.float32
BF16 = jnp.bfloat16

RMS_EPS = 1e-6
N_NORMS = 6
POOL_WINDOWS = (2, 4, 8, 16)
POOL_GW = 256
POOL_HALO = 16
MOBA_HEADS = 8
HEAD_DIM = 128
ROT_DIM = 32
ROPE_THETA = 500000.0
MOBA_BLOCK = 256
MOBA_TOPK = 3
MEM_HEADS = 4
MEM_HEAD_DIM = 256

MLP_TILE = 512
POOL_TILE = 512
QKV_TILE = 512
XATTN_TILE = 1024
ATTN_QBLOCKS = 4
SCORE_LOOKAHEAD = 2
MEMKV_BATCH = 2
XATTN_SUB_ROWS = 256
MLP_SUB_ROWS = 256
FF_CHUNK = 1024
V7X_VMEM_BYTES = 64 * 1024 * 1024
V7X_VMEM_USABLE = 56 * 1024 * 1024

GATE_ROWS = 16
MASK_BIAS = -1e30
SUM_ROWS = 16
LOG2_E = 1.4426950408889634

NT_DIMS = (((1,), (1,)), ((), ()))
TN_DIMS = (((0,), (0,)), ((), ()))


def _nbytes(shape, dtype):
    n = 1
    for d in shape:
        n *= d
    return n * jnp.dtype(dtype).itemsize


def _compiler_params(semantics, pipelined_bytes, resident_bytes, temp_bytes):
    need = 2 * pipelined_bytes + resident_bytes + temp_bytes
    return pltpu.CompilerParams(
        dimension_semantics=semantics,
        vmem_limit_bytes=int(min(max(need, 16 * 1024 * 1024), V7X_VMEM_USABLE)),
    )


def _layer_block(stack, layer, **kwargs):
    shape = (1,) + tuple(stack.shape[1:])
    zeros = (0,) * (stack.ndim - 1)
    return pl.BlockSpec(shape, lambda *_: (layer,) + zeros, **kwargs)


def _sub_tiles(rows, sub_rows):
    assert rows % sub_rows == 0
    return [slice(r, r + sub_rows) for r in range(0, rows, sub_rows)]


def _rms(xf, gain):
    ms = jnp.mean(xf * xf, axis=-1, keepdims=True)
    return xf * lax.rsqrt(ms + RMS_EPS) * gain


def _pool_kernel(x_ref, g_ref, win_ref, wg_ref, scale_ref, o_ref, carry_ref, *, tm, gk):
    s = pl.program_id(1)

    @pl.when(s == 0)
    def _():
        carry_ref[...] = jnp.zeros_like(carry_ref)

    x = x_ref[0]
    hn = _rms(x, g_ref[0, gk:gk + 1, :]).astype(BF16)
    u = jnp.dot(hn, win_ref[0].astype(BF16), preferred_element_type=F32)
    ext = jnp.concatenate([carry_ref[...], u], axis=0)
    carry_ref[...] = u[tm - POOL_HALO:, :]

    pos = s * tm + lax.broadcasted_iota(jnp.int32, (tm, 1), 0)
    ys = []
    for g, w in enumerate(POOL_WINDOWS):
        cols = slice(g * POOL_GW, (g + 1) * POOL_GW)
        t = ext[:, cols]
        shift = 1
        while shift < w:
            t = t + pltpu.roll(t, shift, 0)
            shift *= 2
        win_sum = t[POOL_HALO:, :]
        cnt = jnp.minimum(pos + 1, w).astype(F32)
        pooled = win_sum / cnt - u[:, cols]
        ys.append(jnp.dot(pooled.astype(BF16), wg_ref[0, g].astype(BF16), preferred_element_type=F32))
    y = jnp.concatenate(ys, axis=1) * scale_ref[0]
    o_ref[0] = x + _rms(y, g_ref[0, gk + 1:gk + 2, :])


def _pool_layer(x, gains, layer, gk, w_in, w_group, scale, j):
    b, s, d = x.shape
    tm = POOL_TILE
    assert s % tm == 0 and tm % 8 == 0 and tm >= POOL_HALO
    tile = _nbytes((tm, d), F32)
    weights = _nbytes(w_in.shape[1:], F32) + _nbytes(w_group.shape[1:], F32)
    return pl.pallas_call(
        functools.partial(_pool_kernel, tm=tm, gk=gk),
        grid=(b, s // tm),
        in_specs=[
            pl.BlockSpec((1, tm, d), lambda bi, si: (bi, si, 0)),
            _layer_block(gains, layer),
            _layer_block(w_in, j),
            _layer_block(w_group, j),
            _layer_block(scale, j),
        ],
        out_specs=pl.BlockSpec((1, tm, d), lambda bi, si: (bi, si, 0)),
        out_shape=jax.ShapeDtypeStruct(x.shape, F32),
        scratch_shapes=[pltpu.VMEM((POOL_HALO, d), F32)],
        compiler_params=_compiler_params(("arbitrary", "arbitrary"), 2 * tile + weights, 0, 8 * tile),
        name="pool_sublayer",
    )(x, gains, w_in, w_group, scale)


def _rope_tables(s):
    half = ROT_DIM // 2
    pos = jnp.arange(s, dtype=F32)
    inv_freq = ROPE_THETA ** (-jnp.arange(0, ROT_DIM, 2, dtype=F32) / ROT_DIM)
    ang = pos[:, None] * inv_freq[None, :]
    cos, sin = jnp.cos(ang), jnp.sin(ang)
    pad = HEAD_DIM - ROT_DIM
    c = jnp.concatenate([cos, cos, jnp.ones((s, pad), F32)], axis=1)
    s_lo = jnp.concatenate([-sin, jnp.zeros((s, HEAD_DIM - half), F32)], axis=1)
    s_hi = jnp.concatenate([jnp.zeros((s, half), F32), sin, jnp.zeros((s, pad), F32)], axis=1)
    k_tab = jnp.stack([c, s_lo, s_hi])
    return jnp.concatenate([k_tab * (LOG2_E * HEAD_DIM ** -0.5), k_tab], axis=0)


def _qkv_kernel(x_ref, g_ref, wq_ref, wk_ref, wv_ref, rope_ref, q_ref, k2_ref, vt_ref, kmean_ref, *, tm, gk):
    s = pl.program_id(1)
    half = ROT_DIM // 2
    blocks_per_tile = tm // MOBA_BLOCK
    hn = _rms(x_ref[0], g_ref[0, gk:gk + 1, :]).astype(BF16)
    for c in range(2 * MOBA_HEADS):
        if c % 2 == 0:
            w_ref = wq_ref if c < MOBA_HEADS else wk_ref
            c0 = c % MOBA_HEADS
            w2 = w_ref[0, :, c0 * HEAD_DIM:(c0 + 2) * HEAD_DIM].astype(BF16)
            y2 = jnp.dot(hn, w2, preferred_element_type=F32)
        y = y2[:, (c % 2) * HEAD_DIM:(c % 2 + 1) * HEAD_DIM]
        t = 0 if c < MOBA_HEADS else 3
        y = (y * rope_ref[t]
             + pltpu.roll(y, HEAD_DIM - half, 1) * rope_ref[t + 1]
             + pltpu.roll(y, half, 1) * rope_ref[t + 2])
        if c < MOBA_HEADS:
            q_ref[0, c] = y.astype(BF16)
        else:
            h = c - MOBA_HEADS
            for r in range(blocks_per_tile):
                row = jnp.mean(y[r * MOBA_BLOCK:(r + 1) * MOBA_BLOCK, :], axis=0, keepdims=True)
                kmean_ref[0, h, pl.ds(s * blocks_per_tile + r, 1), :] = row
            if h % 2 == 0:
                k_even = y
            else:
                k2_ref[0, h // 2] = jnp.concatenate([k_even, y], axis=1).astype(BF16)
    vt = lax.dot_general(wv_ref[0].astype(BF16), hn, (((0,), (1,)), ((), ())), preferred_element_type=F32)
    vt_ref[0] = vt.reshape(MOBA_HEADS, HEAD_DIM, tm).astype(BF16)


def _moba_qkv(x, gains, layer, gk, w_qkv, j, rope):
    b, s, d = x.shape
    tm = QKV_TILE
    nb = s // MOBA_BLOCK
    h = MOBA_HEADS
    assert s % tm == 0 and tm % MOBA_BLOCK == 0 and d == h * HEAD_DIM and w_qkv.shape[1:] == (d, 3 * d)
    tile = _nbytes((tm, d), F32)
    out_tile = _nbytes((3 * h, tm, HEAD_DIM), BF16)
    rope_tile = _nbytes((6, tm, HEAD_DIM), F32)
    weights = _nbytes((d, 3 * d), F32)
    w_part = lambda part: pl.BlockSpec((1, d, d), lambda bi, si: (j, 0, part))
    return pl.pallas_call(
        functools.partial(_qkv_kernel, tm=tm, gk=gk),
        grid=(b, s // tm),
        in_specs=[
            pl.BlockSpec((1, tm, d), lambda bi, si: (bi, si, 0)),
            _layer_block(gains, layer),
            w_part(0), w_part(1), w_part(2),
            pl.BlockSpec((6, tm, HEAD_DIM), lambda bi, si: (0, si, 0)),
        ],
        out_specs=[
            pl.BlockSpec((1, h, tm, HEAD_DIM), lambda bi, si: (bi, 0, si, 0)),
            pl.BlockSpec((1, h // 2, tm, 2 * HEAD_DIM), lambda bi, si: (bi, 0, si, 0)),
            pl.BlockSpec((1, h, HEAD_DIM, tm), lambda bi, si: (bi, 0, 0, si)),
            pl.BlockSpec((1, h, nb, HEAD_DIM), lambda bi, si: (bi, 0, 0, 0)),
        ],
        out_shape=[
            jax.ShapeDtypeStruct((b, h, s, HEAD_DIM), BF16),
            jax.ShapeDtypeStruct((b, h // 2, s, 2 * HEAD_DIM), BF16),
            jax.ShapeDtypeStruct((b, h, HEAD_DIM, s), BF16),
            jax.ShapeDtypeStruct((b, h, nb, HEAD_DIM), F32),
        ],
        compiler_params=_compiler_params(
            ("arbitrary", "arbitrary"), tile + out_tile + rope_tile + weights, 0, 4 * tile),
        name="moba_qkv",
    )(x, gains, w_qkv, w_qkv, w_qkv, rope)


def _attn_query_block(q_ref, k2_ref, vt_ref, km_ref, ot_ref, ii, qb):
    blk = MOBA_BLOCK
    hd = HEAD_DIM
    nb = k2_ref.shape[2] // blk
    nbp = GATE_ROWS
    gated = ii > MOBA_TOPK
    q_rows = slice(qb * blk, (qb + 1) * blk)

    key_id = lax.broadcasted_iota(jnp.int32, (blk, 2 * blk), 0)
    qry_id = lax.broadcasted_iota(jnp.int32, (blk, 2 * blk), 1) & (blk - 1)
    causal = key_id <= qry_id
    blk_id = lax.broadcasted_iota(jnp.int32, (nbp, blk), 0)
    zeros_q = jnp.zeros((blk, hd), BF16)
    ones = jnp.ones((SUM_ROWS, blk), BF16)

    def block_bias(q, h):
        km = km_ref[0, h]
        km = jnp.concatenate([km, jnp.zeros((nbp - nb, hd), F32)], axis=0)
        km_hi = km.astype(BF16)
        km_lo = (km - km_hi.astype(F32)).astype(BF16)
        gate = (lax.dot_general(km_hi, q, NT_DIMS, preferred_element_type=F32)
                + lax.dot_general(km_lo, q, NT_DIMS, preferred_element_type=F32))
        rank = jnp.zeros((nbp, blk), F32)
        for jp in range(ii):
            gj = gate[jp:jp + 1, :]
            beats = (gj > gate) | ((gj == gate) & (blk_id > jp))
            rank = rank + beats.astype(F32)
        return jnp.where(rank >= float(MOBA_TOPK), MASK_BIAS, 0.0)

    def pair(g, carry):
        q0 = q_ref[0, 2 * g, q_rows, :]
        q1 = q_ref[0, 2 * g + 1, q_rows, :]
        qd = jnp.concatenate([jnp.concatenate([q0, zeros_q], axis=1),
                              jnp.concatenate([zeros_q, q1], axis=1)], axis=0)
        if gated:
            bias = jnp.concatenate([block_bias(q0, 2 * g), block_bias(q1, 2 * g + 1)], axis=1)
        m = None
        acc = [None, None]
        order = [ii] + list(range(ii))

        def scores(j):
            sc = lax.dot_general(k2_ref[0, g, j * blk:(j + 1) * blk, :], qd, NT_DIMS,
                                 preferred_element_type=F32)
            if j == ii:
                sc = jnp.where(causal, sc, -jnp.inf)
            bm = jnp.max(sc, axis=0, keepdims=True)
            if gated and j < ii:
                bm = bm + bias[j:j + 1, :]
            return sc, bm

        def accumulate(j, p, alpha):
            for hh in range(2):
                cols = slice(hh * blk, (hh + 1) * blk)
                vt1 = jnp.concatenate([vt_ref[0, 2 * g + hh, :, j * blk:(j + 1) * blk], ones], axis=0)
                pv = jnp.dot(vt1, p[:, cols], preferred_element_type=F32)
                acc[hh] = pv if alpha is None else acc[hh] * alpha[:, cols] + pv

        n_blk = len(order)
        sc_q = [scores(j) for j in order[:SCORE_LOOKAHEAD]]
        pending = None
        for t, j in enumerate(order):
            sc, bm = sc_q.pop(0)
            if t + SCORE_LOOKAHEAD < n_blk:
                sc_q.append(scores(order[t + SCORE_LOOKAHEAD]))
            if pending is not None:
                accumulate(*pending)
            masked = gated and j < ii
            m_new = bm if m is None else jnp.maximum(m, bm)
            shift = bias[j:j + 1, :] - m_new if masked else -m_new
            p = jnp.exp2(sc + shift).astype(BF16)
            alpha = None if m is None else jnp.exp2(m - m_new)
            pending = (j, p, alpha)
            m = m_new
        accumulate(*pending)
        for hh in range(2):
            ot_ref[qb, 2 * g + hh] = (acc[hh][:hd, :] / acc[hh][hd:hd + 1, :]).astype(BF16)
        return carry

    lax.fori_loop(0, MOBA_HEADS // 2, pair, 0, unroll=True)


def _attn_kernel(q_ref, k2_ref, vt_ref, km_ref, x_ref, wo_ref, g_ref, o_ref, ot_ref, *, gk):
    i = pl.program_id(1)
    blk = MOBA_BLOCK
    nb = k2_ref.shape[2] // blk
    d = x_ref.shape[2]

    def grid_step(step):
        for qb in range(ATTN_QBLOCKS):
            _attn_query_block(q_ref, k2_ref, vt_ref, km_ref, ot_ref, step * ATTN_QBLOCKS + qb, qb)
        o_t = jnp.concatenate([ot_ref[qb].reshape(d, blk) for qb in range(ATTN_QBLOCKS)], axis=1)
        y = lax.dot_general(o_t, wo_ref[0].astype(BF16), TN_DIMS, preferred_element_type=F32)
        o_ref[0] = x_ref[0] + _rms(y, g_ref[0, gk:gk + 1, :])

    for step in range(nb // ATTN_QBLOCKS):
        pl.when(i == step)(functools.partial(grid_step, step))


def _moba_attn(x, q, k2, vt, kmean, w_o, j, gains, layer, gk):
    b, s, d = x.shape
    blk = MOBA_BLOCK
    nb = s // blk
    h = MOBA_HEADS
    rows = ATTN_QBLOCKS * blk
    assert nb <= GATE_ROWS and blk == 256 and h % 2 == 0 and nb % ATTN_QBLOCKS == 0
    q_tile = _nbytes((h, rows, HEAD_DIM), BF16)
    kv_tile = _nbytes((h, s, HEAD_DIM), BF16)
    x_tile = _nbytes((rows, d), F32)
    return pl.pallas_call(
        functools.partial(_attn_kernel, gk=gk),
        grid=(b, nb // ATTN_QBLOCKS),
        in_specs=[
            pl.BlockSpec((1, h, rows, HEAD_DIM), lambda bi, qi: (bi, 0, qi, 0)),
            pl.BlockSpec((1, h // 2, s, 2 * HEAD_DIM), lambda bi, qi: (bi, 0, 0, 0)),
            pl.BlockSpec((1, h, HEAD_DIM, s), lambda bi, qi: (bi, 0, 0, 0)),
            pl.BlockSpec((1, h, nb, HEAD_DIM), lambda bi, qi: (bi, 0, 0, 0)),
            pl.BlockSpec((1, rows, d), lambda bi, qi: (bi, qi, 0)),
            _layer_block(w_o, j),
            _layer_block(gains, layer),
        ],
        out_specs=pl.BlockSpec((1, rows, d), lambda bi, qi: (bi, qi, 0)),
        out_shape=jax.ShapeDtypeStruct(x.shape, F32),
        scratch_shapes=[pltpu.VMEM((ATTN_QBLOCKS, h, HEAD_DIM, blk), BF16)],
        compiler_params=_compiler_params(
            ("arbitrary", "arbitrary"), q_tile + 2 * kv_tile + 2 * x_tile + _nbytes((d, d), F32),
            x_tile, 8 * _nbytes((blk, s), F32)),
        name="moba_attention",
    )(q, k2, vt, kmean, x, w_o, gains)


def _memkv_kernel(mem_ref, g_ref, w_ref, o_ref):
    nb, m, d = mem_ref.shape
    mn = _rms(mem_ref[...].reshape(nb * m, d), g_ref[0]).astype(BF16)
    kv = jnp.dot(mn, w_ref[0].astype(BF16), preferred_element_type=F32).astype(BF16)
    o_ref[0] = kv.reshape(nb, m, kv.shape[1])


def _mem_kv(mem, mem_norm, w_kv):
    depth, d, d2 = w_kv.shape
    b, m, _ = mem.shape
    nb = MEMKV_BATCH
    assert b % nb == 0 and m % 8 == 0
    return pl.pallas_call(
        _memkv_kernel,
        grid=(depth, b // nb),
        in_specs=[
            pl.BlockSpec((nb, m, d), lambda li, bi: (bi, 0, 0)),
            pl.BlockSpec((1, 1, d), lambda li, bi: (li, 0, 0)),
            pl.BlockSpec((1, d, d2), lambda li, bi: (li, 0, 0)),
        ],
        out_specs=pl.BlockSpec((1, nb, m, d2), lambda li, bi: (li, bi, 0, 0)),
        out_shape=jax.ShapeDtypeStruct((depth, b, m, d2), BF16),
        compiler_params=_compiler_params(
            ("arbitrary", "arbitrary"),
            _nbytes((nb * m, d), F32) + _nbytes((d, d2), F32) + _nbytes((nb * m, d2), BF16), 0,
            3 * _nbytes((nb * m, d2), F32)),
        name="memory_kv",
    )(mem, mem_norm.reshape(depth, 1, d), w_kv)


def _xattn_kernel(x_ref, g_ref, wq_ref, kv_ref, wo_ref, o_ref, *, gk):
    d = x_ref.shape[2]
    hd = MEM_HEAD_DIM
    subs = _sub_tiles(x_ref.shape[1], XATTN_SUB_ROWS)
    heads = range(MEM_HEADS)
    xs = [x_ref[0, rows, :] for rows in subs]
    hn = [_rms(x, g_ref[0, gk:gk + 1, :]).astype(BF16) for x in xs]
    wq = wq_ref[0].astype(BF16)
    q_scale = LOG2_E * hd ** -0.5
    q = [(jnp.dot(h, wq, preferred_element_type=F32) * q_scale).astype(BF16) for h in hn]
    sc = [[lax.dot_general(qs[:, h * hd:(h + 1) * hd], kv_ref[0, 0, :, h * hd:(h + 1) * hd], NT_DIMS,
                           preferred_element_type=F32) for h in heads] for qs in q]
    p, inv_l = [], []
    for sc_s in sc:
        e = [jnp.exp2(t - jnp.max(t, axis=1, keepdims=True)) for t in sc_s]
        inv_l.append([1.0 / jnp.sum(t, axis=1, keepdims=True) for t in e])
        p.append([t.astype(BF16) for t in e])
    wo = wo_ref[0].astype(BF16)
    for rows, x, p_s, inv_s in zip(subs, xs, p, inv_l):
        outs = [jnp.dot(p_s[h], kv_ref[0, 0, :, d + h * hd:d + (h + 1) * hd], preferred_element_type=F32) * inv_s[h]
                for h in heads]
        o = jnp.concatenate(outs, axis=1).astype(BF16)
        y = jnp.dot(o, wo, preferred_element_type=F32)
        o_ref[0, rows, :] = x + _rms(y, g_ref[0, gk + 1:gk + 2, :])


def _xattn_layer(x, gains, layer, gk, w_q, kv_all, w_o):
    b, s, d = x.shape
    tm = XATTN_TILE
    m = kv_all.shape[2]
    assert s % tm == 0 and d == MEM_HEADS * MEM_HEAD_DIM
    tile = _nbytes((tm, d), F32)
    return pl.pallas_call(
        functools.partial(_xattn_kernel, gk=gk),
        grid=(b, s // tm),
        in_specs=[
            pl.BlockSpec((1, tm, d), lambda bi, si: (bi, si, 0)),
            _layer_block(gains, layer),
            _layer_block(w_q, layer),
            pl.BlockSpec((1, 1, m, 2 * d), lambda bi, si: (layer, bi, 0, 0)),
            _layer_block(w_o, layer),
        ],
        out_specs=pl.BlockSpec((1, tm, d), lambda bi, si: (bi, si, 0)),
        out_shape=jax.ShapeDtypeStruct(x.shape, F32),
        compiler_params=_compiler_params(
            ("arbitrary", "arbitrary"),
            2 * tile + 2 * _nbytes((d, d), F32) + _nbytes((m, 2 * d), BF16), 0, 6 * tile),
        name="memory_xattn_sublayer",
    )(x, gains, w_q, kv_all, w_o)


def _mlp_kernel(x_ref, g_ref, w1_hbm, w2_hbm, o_ref, w1_ref, w2_ref, sem, *, gk, layer):
    d_ff = w1_ref.shape[1]
    n_chunks = d_ff // FF_CHUNK
    step = pl.program_id(0)

    def chunk_copies(c):
        ff = pl.ds(c * FF_CHUNK, FF_CHUNK)
        return (pltpu.make_async_copy(w1_hbm.at[layer, :, ff], w1_ref.at[:, ff], sem.at[0, c]),
                pltpu.make_async_copy(w2_hbm.at[layer, ff, :], w2_ref.at[ff, :], sem.at[1, c]))

    @pl.when(step == 0)
    def _():
        for c in range(n_chunks):
            for copy in chunk_copies(c):
                copy.start()

    def body(wait_for_weights):
        subs = _sub_tiles(x_ref.shape[0], MLP_SUB_ROWS)
        xs = [x_ref[rows, :] for rows in subs]
        hn = [_rms(x, g_ref[0, gk:gk + 1, :]).astype(BF16) for x in xs]
        acc = [jnp.zeros(x.shape, F32) for x in xs]
        for c in range(n_chunks):
            if wait_for_weights:
                for copy in chunk_copies(c):
                    copy.wait()
            cols = slice(c * FF_CHUNK, (c + 1) * FF_CHUNK)
            w1c = w1_ref[:, cols].astype(BF16)
            w2c = w2_ref[cols, :].astype(BF16)
            a = [jnp.maximum(jnp.dot(h, w1c, preferred_element_type=F32), 0.0) for h in hn]
            acc = [t + jnp.dot((u * u).astype(BF16), w2c, preferred_element_type=F32) for t, u in zip(acc, a)]
        for rows, x, t in zip(subs, xs, acc):
            o_ref[rows, :] = x + _rms(t, g_ref[0, gk + 1:gk + 2, :])

    pl.when(step == 0)(functools.partial(body, True))
    pl.when(step != 0)(functools.partial(body, False))


def _mlp_layer(x, gains, layer, gk, w1, w2):
    b, s, d = x.shape
    tm = MLP_TILE
    rows = b * s
    d_ff = w1.shape[2]
    assert rows % tm == 0 and d_ff % FF_CHUNK == 0
    tile = _nbytes((tm, d), F32)
    weights = 2 * _nbytes((d, d_ff), F32)
    out = pl.pallas_call(
        functools.partial(_mlp_kernel, gk=gk, layer=layer),
        grid=(rows // tm,),
        in_specs=[
            pl.BlockSpec((tm, d), lambda ri: (ri, 0)),
            _layer_block(gains, layer),
            pl.BlockSpec(memory_space=pl.ANY),
            pl.BlockSpec(memory_space=pl.ANY),
        ],
        out_specs=pl.BlockSpec((tm, d), lambda ri: (ri, 0)),
        out_shape=jax.ShapeDtypeStruct((rows, d), F32),
        scratch_shapes=[
            pltpu.VMEM((d, d_ff), F32),
            pltpu.VMEM((d_ff, d), F32),
            pltpu.SemaphoreType.DMA((2, d_ff // FF_CHUNK)),
        ],
        compiler_params=_compiler_params(("arbitrary",), 2 * tile, weights, 6 * tile),
        name="mlp_sublayer",
    )(x.reshape(rows, d), gains, w1, w2)
    return out.reshape(b, s, d)


def kernel(x, mem, norm_gains, mem_norm, pool_w_in, pool_w_group, pool_scale, moba_w_qkv, moba_w_o,
           xa_w_q, xa_w_kv, xa_w_o, mlp_w1, mlp_w2):
    depth = norm_gains.shape[0]
    s, d = x.shape[1], x.shape[2]
    assert norm_gains.shape[1] == N_NORMS
    rope = _rope_tables(s)
    scale = pool_scale.reshape(pool_scale.shape[0], 1, d)
    kv_all = _mem_kv(mem, mem_norm, xa_w_kv)
    for i in range(depth):
        j = i // 2
        if i % 2 == 0:
            x = _pool_layer(x, norm_gains, i, 0, pool_w_in, pool_w_group, scale, j)
        else:
            q, k2, vt, kmean = _moba_qkv(x, norm_gains, i, 0, moba_w_qkv, j, rope)
            x = _moba_attn(x, q, k2, vt, kmean, moba_w_o, j, norm_gains, i, 1)
        x = _xattn_layer(x, norm_gains, i, 2, xa_w_q, kv_all, xa_w_o)
        x = _mlp_layer(x, norm_gains, i, 4, mlp_w1, mlp_w2)
    return x
```

```python
import functools

import jax
import jax.numpy as jnp
from jax import lax
from jax.experimental import pallas as pl
from jax.experimental.pallas import tpu as pltpu

F32 = jnp.float32
BF16 = jnp.bfloat16

RMS_EPS = 1e-6
N_NORMS = 6
POOL_WINDOWS = (2, 4, 8, 16)
POOL_GW = 256
POOL_HALO = 16
MOBA_HEADS = 8
HEAD_DIM = 128
ROT_DIM = 32
ROPE_THETA = 500000.0
MOBA_BLOCK = 256
MOBA_TOPK = 3
MEM_HEADS = 4
MEM_HEAD_DIM = 256

MLP_TILE = 512
POOL_TILE = 512
QKV_TILE = 512
XATTN_TILE = 1024
ATTN_QBLOCKS = 4
SCORE_LOOKAHEAD = 2
MEMKV_BATCH = 4
XATTN_SUB_ROWS = 256
MLP_SUB_ROWS = 256
FF_CHUNK = 1024
V7X_VMEM_BYTES = 64 * 1024 * 1024
V7X_VMEM_USABLE = 56 * 1024 * 1024

GATE_ROWS = 16
MASK_BIAS = -1e30
SUM_ROWS = 16
LOG2_E = 1.4426950408889634

NT_DIMS = (((1,), (1,)), ((), ()))
TN_DIMS = (((0,), (0,)), ((), ()))


def _nbytes(shape, dtype):
    n = 1
    for d in shape:
        n *= d
    return n * jnp.dtype(dtype).itemsize


def _compiler_params(semantics, pipelined_bytes, resident_bytes, temp_bytes):
    need = 2 * pipelined_bytes + resident_bytes + temp_bytes
    return pltpu.CompilerParams(
        dimension_semantics=semantics,
        vmem_limit_bytes=int(min(max(need, 16 * 1024 * 1024), V7X_VMEM_USABLE)),
    )


def _layer_block(stack, layer, **kwargs):
    shape = (1,) + tuple(stack.shape[1:])
    zeros = (0,) * (stack.ndim - 1)
    return pl.BlockSpec(shape, lambda *_: (layer,) + zeros, **kwargs)


def _sub_tiles(rows, sub_rows):
    assert rows % sub_rows == 0
    return [slice(r, r + sub_rows) for r in range(0, rows, sub_rows)]


def _rms(xf, gain):
    ms = jnp.mean(xf * xf, axis=-1, keepdims=True)
    return xf * lax.rsqrt(ms + RMS_EPS) * gain


def _pool_kernel(x_ref, g_ref, win_ref, wg_ref, scale_ref, o_ref, carry_ref, *, tm, gk):
    s = pl.program_id(1)

    @pl.when(s == 0)
    def _():
        carry_ref[...] = jnp.zeros_like(carry_ref)

    x = x_ref[0]
    hn = _rms(x, g_ref[0, gk:gk + 1, :]).astype(BF16)
    u = jnp.dot(hn, win_ref[0].astype(BF16), preferred_element_type=F32)
    ext = jnp.concatenate([carry_ref[...], u], axis=0)
    carry_ref[...] = u[tm - POOL_HALO:, :]

    pos = s * tm + lax.broadcasted_iota(jnp.int32, (tm, 1), 0)
    ys = []
    for g, w in enumerate(POOL_WINDOWS):
        cols = slice(g * POOL_GW, (g + 1) * POOL_GW)
        t = ext[:, cols]
        shift = 1
        while shift < w:
            t = t + pltpu.roll(t, shift, 0)
            shift *= 2
        win_sum = t[POOL_HALO:, :]
        cnt = jnp.minimum(pos + 1, w).astype(F32)
        pooled = win_sum / cnt - u[:, cols]
        ys.append(jnp.dot(pooled.astype(BF16), wg_ref[0, g].astype(BF16), preferred_element_type=F32))
    y = jnp.concatenate(ys, axis=1) * scale_ref[0]
    o_ref[0] = x + _rms(y, g_ref[0, gk + 1:gk + 2, :])


def _pool_layer(x, gains, layer, gk, w_in, w_group, scale, j):
    b, s, d = x.shape
    tm = POOL_TILE
    assert s % tm == 0 and tm % 8 == 0 and tm >= POOL_HALO
    tile = _nbytes((tm, d), F32)
    weights = _nbytes(w_in.shape[1:], F32) + _nbytes(w_group.shape[1:], F32)
    return pl.pallas_call(
        functools.partial(_pool_kernel, tm=tm, gk=gk),
        grid=(b, s // tm),
        in_specs=[
            pl.BlockSpec((1, tm, d), lambda bi, si: (bi, si, 0)),
            _layer_block(gains, layer),
            _layer_block(w_in, j),
            _layer_block(w_group, j),
            _layer_block(scale, j),
        ],
        out_specs=pl.BlockSpec((1, tm, d), lambda bi, si: (bi, si, 0)),
        out_shape=jax.ShapeDtypeStruct(x.shape, F32),
        scratch_shapes=[pltpu.VMEM((POOL_HALO, d), F32)],
        compiler_params=_compiler_params(("arbitrary", "arbitrary"), 2 * tile + weights, 0, 8 * tile),
        name="pool_sublayer",
    )(x, gains, w_in, w_group, scale)


def _rope_tables(s):
    half = ROT_DIM // 2
    pos = jnp.arange(s, dtype=F32)
    inv_freq = ROPE_THETA ** (-jnp.arange(0, ROT_DIM, 2, dtype=F32) / ROT_DIM)
    ang = pos[:, None] * inv_freq[None, :]
    cos, sin = jnp.cos(ang), jnp.sin(ang)
    pad = HEAD_DIM - ROT_DIM
    c = jnp.concatenate([cos, cos, jnp.ones((s, pad), F32)], axis=1)
    s_lo = jnp.concatenate([-sin, jnp.zeros((s, HEAD_DIM - half), F32)], axis=1)
    s_hi = jnp.concatenate([jnp.zeros((s, half), F32), sin, jnp.zeros((s, pad), F32)], axis=1)
    k_tab = jnp.stack([c, s_lo, s_hi])
    return jnp.concatenate([k_tab * (LOG2_E * HEAD_DIM ** -0.5), k_tab], axis=0)


def _qkv_kernel(x_ref, g_ref, wq_ref, wk_ref, wv_ref, rope_ref, q_ref, k2_ref, vt_ref, kmean_ref, *, tm, gk):
    s = pl.program_id(1)
    half = ROT_DIM // 2
    blocks_per_tile = tm // MOBA_BLOCK
    hn = _rms(x_ref[0], g_ref[0, gk:gk + 1, :]).astype(BF16)
    for c in range(2 * MOBA_HEADS):
        if c % 2 == 0:
            w_ref = wq_ref if c < MOBA_HEADS else wk_ref
            c0 = c % MOBA_HEADS
            w2 = w_ref[0, :, c0 * HEAD_DIM:(c0 + 2) * HEAD_DIM].astype(BF16)
            y2 = jnp.dot(hn, w2, preferred_element_type=F32)
        y = y2[:, (c % 2) * HEAD_DIM:(c % 2 + 1) * HEAD_DIM]
        t = 0 if c < MOBA_HEADS else 3
        y = (y * rope_ref[t]
             + pltpu.roll(y, HEAD_DIM - half, 1) * rope_ref[t + 1]
             + pltpu.roll(y, half, 1) * rope_ref[t + 2])
        if c < MOBA_HEADS:
            q_ref[0, c] = y.astype(BF16)
        else:
            h = c - MOBA_HEADS
            for r in range(blocks_per_tile):
                row = jnp.mean(y[r * MOBA_BLOCK:(r + 1) * MOBA_BLOCK, :], axis=0, keepdims=True)
                kmean_ref[0, h, pl.ds(s * blocks_per_tile + r, 1), :] = row
            if h % 2 == 0:
                k_even = y
            else:
                k2_ref[0, h // 2] = jnp.concatenate([k_even, y], axis=1).astype(BF16)
    vt = lax.dot_general(wv_ref[0].astype(BF16), hn, (((0,), (1,)), ((), ())), preferred_element_type=F32)
    vt_ref[0] = vt.reshape(MOBA_HEADS, HEAD_DIM, tm).astype(BF16)


def _moba_qkv(x, gains, layer, gk, w_qkv, j, rope):
    b, s, d = x.shape
    tm = QKV_TILE
    nb = s // MOBA_BLOCK
    h = MOBA_HEADS
    assert s % tm == 0 and tm % MOBA_BLOCK == 0 and d == h * HEAD_DIM and w_qkv.shape[1:] == (d, 3 * d)
    tile = _nbytes((tm, d), F32)
    out_tile = _nbytes((3 * h, tm, HEAD_DIM), BF16)
    rope_tile = _nbytes((6, tm, HEAD_DIM), F32)
    weights = _nbytes((d, 3 * d), F32)
    w_part = lambda part: pl.BlockSpec((1, d, d), lambda bi, si: (j, 0, part))
    return pl.pallas_call(
        functools.partial(_qkv_kernel, tm=tm, gk=gk),
        grid=(b, s // tm),
        in_specs=[
            pl.BlockSpec((1, tm, d), lambda bi, si: (bi, si, 0)),
            _layer_block(gains, layer),
            w_part(0), w_part(1), w_part(2),
            pl.BlockSpec((6, tm, HEAD_DIM), lambda bi, si: (0, si, 0)),
        ],
        out_specs=[
            pl.BlockSpec((1, h, tm, HEAD_DIM), lambda bi, si: (bi, 0, si, 0)),
            pl.BlockSpec((1, h // 2, tm, 2 * HEAD_DIM), lambda bi, si: (bi, 0, si, 0)),
            pl.BlockSpec((1, h, HEAD_DIM, tm), lambda bi, si: (bi, 0, 0, si)),
            pl.BlockSpec((1, h, nb, HEAD_DIM), lambda bi, si: (bi, 0, 0, 0)),
        ],
        out_shape=[
            jax.ShapeDtypeStruct((b, h, s, HEAD_DIM), BF16),
            jax.ShapeDtypeStruct((b, h // 2, s, 2 * HEAD_DIM), BF16),
            jax.ShapeDtypeStruct((b, h, HEAD_DIM, s), BF16),
            jax.ShapeDtypeStruct((b, h, nb, HEAD_DIM), F32),
        ],
        compiler_params=_compiler_params(
            ("arbitrary", "arbitrary"), tile + out_tile + rope_tile + weights, 0, 4 * tile),
        name="moba_qkv",
    )(x, gains, w_qkv, w_qkv, w_qkv, rope)


def _attn_query_block(q_ref, k2_ref, vt_ref, km_ref, ot_ref, ii, qb):
    blk = MOBA_BLOCK
    hd = HEAD_DIM
    nb = k2_ref.shape[2] // blk
    nbp = GATE_ROWS
    gated = ii > MOBA_TOPK
    q_rows = slice(qb * blk, (qb + 1) * blk)

    key_id = lax.broadcasted_iota(jnp.int32, (blk, 2 * blk), 0)
    qry_id = lax.broadcasted_iota(jnp.int32, (blk, 2 * blk), 1) & (blk - 1)
    causal = key_id <= qry_id
    blk_id = lax.broadcasted_iota(jnp.int32, (nbp, blk), 0)
    zeros_q = jnp.zeros((blk, hd), BF16)
    ones = jnp.ones((SUM_ROWS, blk), BF16)

    def block_bias(q, h):
        km = km_ref[0, h]
        km = jnp.concatenate([km, jnp.zeros((nbp - nb, hd), F32)], axis=0)
        km_hi = km.astype(BF16)
        km_lo = (km - km_hi.astype(F32)).astype(BF16)
        gate = (lax.dot_general(km_hi, q, NT_DIMS, preferred_element_type=F32)
                + lax.dot_general(km_lo, q, NT_DIMS, preferred_element_type=F32))
        rank = jnp.zeros((nbp, blk), F32)
        for jp in range(ii):
            gj = gate[jp:jp + 1, :]
            beats = (gj > gate) | ((gj == gate) & (blk_id > jp))
            rank = rank + beats.astype(F32)
        return jnp.where(rank >= float(MOBA_TOPK), MASK_BIAS, 0.0)

    def pair(g, carry):
        q0 = q_ref[0, 2 * g, q_rows, :]
        q1 = q_ref[0, 2 * g + 1, q_rows, :]
        qd = jnp.concatenate([jnp.concatenate([q0, zeros_q], axis=1),
                              jnp.concatenate([zeros_q, q1], axis=1)], axis=0)
        if gated:
            bias = jnp.concatenate([block_bias(q0, 2 * g), block_bias(q1, 2 * g + 1)], axis=1)
        m = None
        acc = [None, None]
        order = [ii] + list(range(ii))

        def scores(j):
            sc = lax.dot_general(k2_ref[0, g, j * blk:(j + 1) * blk, :], qd, NT_DIMS,
                                 preferred_element_type=F32)
            if j == ii:
                sc = jnp.where(causal, sc, -jnp.inf)
            bm = jnp.max(sc, axis=0, keepdims=True)
            if gated and j < ii:
                bm = bm + bias[j:j + 1, :]
            return sc, bm

        def accumulate(j, p, alpha):
            for hh in range(2):
                cols = slice(hh * blk, (hh + 1) * blk)
                vt1 = jnp.concatenate([vt_ref[0, 2 * g + hh, :, j * blk:(j + 1) * blk], ones], axis=0)
                pv = jnp.dot(vt1, p[:, cols], preferred_element_type=F32)
                acc[hh] = pv if alpha is None else acc[hh] * alpha[:, cols] + pv

        n_blk = len(order)
        sc_q = [scores(j) for j in order[:SCORE_LOOKAHEAD]]
        pending = None
        for t, j in enumerate(order):
            sc, bm = sc_q.pop(0)
            if t + SCORE_LOOKAHEAD < n_blk:
                sc_q.append(scores(order[t + SCORE_LOOKAHEAD]))
            if pending is not None:
                accumulate(*pending)
            masked = gated and j < ii
            m_new = bm if m is None else jnp.maximum(m, bm)
            shift = bias[j:j + 1, :] - m_new if masked else -m_new
            p = jnp.exp2(sc + shift).astype(BF16)
            alpha = None if m is None else jnp.exp2(m - m_new)
            pending = (j, p, alpha)
            m = m_new
        accumulate(*pending)
        for hh in range(2):
            ot_ref[qb, 2 * g + hh] = (acc[hh][:hd, :] / acc[hh][hd:hd + 1, :]).astype(BF16)
        return carry

    lax.fori_loop(0, MOBA_HEADS // 2, pair, 0, unroll=True)


def _attn_kernel(q_ref, k2_ref, vt_ref, km_ref, x_ref, wo_ref, g_ref, o_ref, ot_ref, *, gk):
    i = pl.program_id(1)
    blk = MOBA_BLOCK
    nb = k2_ref.shape[2] // blk
    d = x_ref.shape[2]

    def grid_step(step):
        for qb in range(ATTN_QBLOCKS):
            _attn_query_block(q_ref, k2_ref, vt_ref, km_ref, ot_ref, step * ATTN_QBLOCKS + qb, qb)
        o_t = jnp.concatenate([ot_ref[qb].reshape(d, blk) for qb in range(ATTN_QBLOCKS)], axis=1)
        y = lax.dot_general(o_t, wo_ref[0].astype(BF16), TN_DIMS, preferred_element_type=F32)
        o_ref[0] = x_ref[0] + _rms(y, g_ref[0, gk:gk + 1, :])

    for step in range(nb // ATTN_QBLOCKS):
        pl.when(i == step)(functools.partial(grid_step, step))


def _moba_attn(x, q, k2, vt, kmean, w_o, j, gains, layer, gk):
    b, s, d = x.shape
    blk = MOBA_BLOCK
    nb = s // blk
    h = MOBA_HEADS
    rows = ATTN_QBLOCKS * blk
    assert nb <= GATE_ROWS and blk == 256 and h % 2 == 0 and nb % ATTN_QBLOCKS == 0
    q_tile = _nbytes((h, rows, HEAD_DIM), BF16)
    kv_tile = _nbytes((h, s, HEAD_DIM), BF16)
    x_tile = _nbytes((rows, d), F32)
    return pl.pallas_call(
        functools.partial(_attn_kernel, gk=gk),
        grid=(b, nb // ATTN_QBLOCKS),
        in_specs=[
            pl.BlockSpec((1, h, rows, HEAD_DIM), lambda bi, qi: (bi, 0, qi, 0)),
            pl.BlockSpec((1, h // 2, s, 2 * HEAD_DIM), lambda bi, qi: (bi, 0, 0, 0)),
            pl.BlockSpec((1, h, HEAD_DIM, s), lambda bi, qi: (bi, 0, 0, 0)),
            pl.BlockSpec((1, h, nb, HEAD_DIM), lambda bi, qi: (bi, 0, 0, 0)),
            pl.BlockSpec((1, rows, d), lambda bi, qi: (bi, qi, 0)),
            _layer_block(w_o, j),
            _layer_block(gains, layer),
        ],
        out_specs=pl.BlockSpec((1, rows, d), lambda bi, qi: (bi, qi, 0)),
        out_shape=jax.ShapeDtypeStruct(x.shape, F32),
        scratch_shapes=[pltpu.VMEM((ATTN_QBLOCKS, h, HEAD_DIM, blk), BF16)],
        compiler_params=_compiler_params(
            ("arbitrary", "arbitrary"), q_tile + 2 * kv_tile + 2 * x_tile + _nbytes((d, d), F32),
            x_tile, 8 * _nbytes((blk, s), F32)),
        name="moba_attention",
    )(q, k2, vt, kmean, x, w_o, gains)


def _memkv_kernel(mem_ref, g_ref, w_ref, o_ref):
    nb, m, d = mem_ref.shape
    mn = _rms(mem_ref[...].reshape(nb * m, d), g_ref[0]).astype(BF16)
    kv = jnp.dot(mn, w_ref[0].astype(BF16), preferred_element_type=F32).astype(BF16)
    o_ref[0] = kv.reshape(nb, m, kv.shape[1])


def _mem_kv(mem, mem_norm, w_kv):
    depth, d, d2 = w_kv.shape
    b, m, _ = mem.shape
    nb = MEMKV_BATCH
    assert b % nb == 0 and m % 8 == 0
    return pl.pallas_call(
        _memkv_kernel,
        grid=(depth, b // nb),
        in_specs=[
            pl.BlockSpec((nb, m, d), lambda li, bi: (bi, 0, 0)),
            pl.BlockSpec((1, 1, d), lambda li, bi: (li, 0, 0)),
            pl.BlockSpec((1, d, d2), lambda li, bi: (li, 0, 0)),
        ],
        out_specs=pl.BlockSpec((1, nb, m, d2), lambda li, bi: (li, bi, 0, 0)),
        out_shape=jax.ShapeDtypeStruct((depth, b, m, d2), BF16),
        compiler_params=_compiler_params(
            ("arbitrary", "arbitrary"),
            _nbytes((nb * m, d), F32) + _nbytes((d, d2), F32) + _nbytes((nb * m, d2), BF16), 0,
            3 * _nbytes((nb * m, d2), F32)),
        name="memory_kv",
    )(mem, mem_norm.reshape(depth, 1, d), w_kv)


def _xattn_kernel(x_ref, g_ref, wq_ref, kv_ref, wo_ref, o_ref, *, gk):
    d = x_ref.shape[2]
    hd = MEM_HEAD_DIM
    subs = _sub_tiles(x_ref.shape[1], XATTN_SUB_ROWS)
    heads = range(MEM_HEADS)
    xs = [x_ref[0, rows, :] for rows in subs]
    hn = [_rms(x, g_ref[0, gk:gk + 1, :]).astype(BF16) for x in xs]
    wq = wq_ref[0].astype(BF16)
    q_scale = LOG2_E * hd ** -0.5
    q = [(jnp.dot(h, wq, preferred_element_type=F32) * q_scale).astype(BF16) for h in hn]
    sc = [[lax.dot_general(qs[:, h * hd:(h + 1) * hd], kv_ref[0, 0, :, h * hd:(h + 1) * hd], NT_DIMS,
                           preferred_element_type=F32) for h in heads] for qs in q]
    p, inv_l = [], []
    for sc_s in sc:
        e = [jnp.exp2(t - jnp.max(t, axis=1, keepdims=True)) for t in sc_s]
        inv_l.append([1.0 / jnp.sum(t, axis=1, keepdims=True) for t in e])
        p.append([t.astype(BF16) for t in e])
    wo = wo_ref[0].astype(BF16)
    for rows, x, p_s, inv_s in zip(subs, xs, p, inv_l):
        outs = [jnp.dot(p_s[h], kv_ref[0, 0, :, d + h * hd:d + (h + 1) * hd], preferred_element_type=F32) * inv_s[h]
                for h in heads]
        o = jnp.concatenate(outs, axis=1).astype(BF16)
        y = jnp.dot(o, wo, preferred_element_type=F32)
        o_ref[0, rows, :] = x + _rms(y, g_ref[0, gk + 1:gk + 2, :])


def _xattn_layer(x, gains, layer, gk, w_q, kv_all, w_o):
    b, s, d = x.shape
    tm = XATTN_TILE
    m = kv_all.shape[2]
    assert s % tm == 0 and d == MEM_HEADS * MEM_HEAD_DIM
    tile = _nbytes((tm, d), F32)
    return pl.pallas_call(
        functools.partial(_xattn_kernel, gk=gk),
        grid=(b, s // tm),
        in_specs=[
            pl.BlockSpec((1, tm, d), lambda bi, si: (bi, si, 0)),
            _layer_block(gains, layer),
            _layer_block(w_q, layer),
            pl.BlockSpec((1, 1, m, 2 * d), lambda bi, si: (layer, bi, 0, 0)),
            _layer_block(w_o, layer),
        ],
        out_specs=pl.BlockSpec((1, tm, d), lambda bi, si: (bi, si, 0)),
        out_shape=jax.ShapeDtypeStruct(x.shape, F32),
        compiler_params=_compiler_params(
            ("arbitrary", "arbitrary"),
            2 * tile + 2 * _nbytes((d, d), F32) + _nbytes((m, 2 * d), BF16), 0, 6 * tile),
        name="memory_xattn_sublayer",
    )(x, gains, w_q, kv_all, w_o)


def _mlp_kernel(x_ref, g_ref, w1_hbm, w2_hbm, o_ref, w1_ref, w2_ref, sem, *, gk, layer):
    d_ff = w1_ref.shape[1]
    n_chunks = d_ff // FF_CHUNK
    step = pl.program_id(0)

    def chunk_copies(c):
        ff = pl.ds(c * FF_CHUNK, FF_CHUNK)
        return (pltpu.make_async_copy(w1_hbm.at[layer, :, ff], w1_ref.at[:, ff], sem.at[0, c]),
                pltpu.make_async_copy(w2_hbm.at[layer, ff, :], w2_ref.at[ff, :], sem.at[1, c]))

    @pl.when(step == 0)
    def _():
        for c in range(n_chunks):
            for copy in chunk_copies(c):
                copy.start()

    def body(wait_for_weights):
        subs = _sub_tiles(x_ref.shape[0], MLP_SUB_ROWS)
        xs = [x_ref[rows, :] for rows in subs]
        hn = [_rms(x, g_ref[0, gk:gk + 1, :]).astype(BF16) for x in xs]
        acc = [jnp.zeros(x.shape, F32) for x in xs]
        for c in range(n_chunks):
            if wait_for_weights:
                for copy in chunk_copies(c):
                    copy.wait()
            cols = slice(c * FF_CHUNK, (c + 1) * FF_CHUNK)
            w1c = w1_ref[:, cols].astype(BF16)
            w2c = w2_ref[cols, :].astype(BF16)
            a = [jnp.maximum(jnp.dot(h, w1c, preferred_element_type=F32), 0.0) for h in hn]
            acc = [t + jnp.dot((u * u).astype(BF16), w2c, preferred_element_type=F32) for t, u in zip(acc, a)]
        for rows, x, t in zip(subs, xs, acc):
            o_ref[rows, :] = x + _rms(t, g_ref[0, gk + 1:gk + 2, :])

    pl.when(step == 0)(functools.partial(body, True))
    pl.when(step != 0)(functools.partial(body, False))


def _mlp_layer(x, gains, layer, gk, w1, w2):
    b, s, d = x.shape
    tm = MLP_TILE
    rows = b * s
    d_ff = w1.shape[2]
    assert rows % tm == 0 and d_ff % FF_CHUNK == 0
    tile = _nbytes((tm, d), F32)
    weights = 2 * _nbytes((d, d_ff), F32)
    out = pl.pallas_call(
        functools.partial(_mlp_kernel, gk=gk, layer=layer),
        grid=(rows // tm,),
        in_specs=[
            pl.BlockSpec((tm, d), lambda ri: (ri, 0)),
            _layer_block(gains, layer),
            pl.BlockSpec(memory_space=pl.ANY),
            pl.BlockSpec(memory_space=pl.ANY),
        ],
        out_specs=pl.BlockSpec((tm, d), lambda ri: (ri, 0)),
        out_shape=jax.ShapeDtypeStruct((rows, d), F32),
        scratch_shapes=[
            pltpu.VMEM((d, d_ff), F32),
            pltpu.VMEM((d_ff, d), F32),
            pltpu.SemaphoreType.DMA((2, d_ff // FF_CHUNK)),
        ],
        compiler_params=_compiler_params(("arbitrary",), 2 * tile, weights, 6 * tile),
        name="mlp_sublayer",
    )(x.reshape(rows, d), gains, w1, w2)
    return out.reshape(b, s, d)


def kernel(x, mem, norm_gains, mem_norm, pool_w_in, pool_w_group, pool_scale, moba_w_qkv, moba_w_o,
           xa_w_q, xa_w_kv, xa_w_o, mlp_w1, mlp_w2):
    depth = norm_gains.shape[0]
    s, d = x.shape[1], x.shape[2]
    assert norm_gains.shape[1] == N_NORMS
    rope = _rope_tables(s)
    scale = pool_scale.reshape(pool_scale.shape[0], 1, d)
    kv_all = _mem_kv(mem, mem_norm, xa_w_kv)
    for i in range(depth):
        j = i // 2
        if i % 2 == 0:
            x = _pool_layer(x, norm_gains, i, 0, pool_w_in, pool_w_group, scale, j)
        else:
            q, k2, vt, kmean = _moba_qkv(x, norm_gains, i, 0, moba_w_qkv, j, rope)
            x = _moba_attn(x, q, k2, vt, kmean, moba_w_o, j, norm_gains, i, 1)
        x = _xattn_layer(x, norm_gains, i, 2, xa_w_q, kv_all, xa_w_o)
        x = _mlp_layer(x, norm_gains, i, 4, mlp_w1, mlp_w2)
    return x
```
